```python
import math
import jax, jax.numpy as jnp
from jax import lax
import numpy as np

D_MODEL = 1024
BATCH = 2
SEQ = 8192
DEPTH = 1

N_MEM = 256
D_LRU = D_MODEL // 2
D_CONF = D_MODEL // 2
D_MIX = D_LRU + D_CONF
LRU_HEADS = 8
LRU_HD = D_LRU // LRU_HEADS
LRU_CONV = 4
RG_C = 8.0
CONF_CONV = 31
CONF_GROUPS = 8
XA_HEADS = 4
XA_HD = D_MODEL // XA_HEADS
D_FF = 3 * D_MODEL
FFN_CONV = 3
EPS = 1e-6

kernel_name = "hybrid_rglru_conformer_xattn_convffn"


def rms_norm(x, g):
    xf = x.astype(jnp.float32)
    y = xf * lax.rsqrt(jnp.mean(xf * xf, axis=-1, keepdims=True) + EPS)
    return y.astype(x.dtype) * g


def layer_norm(x, g, b):
    xf = x.astype(jnp.float32)
    mu = jnp.mean(xf, axis=-1, keepdims=True)
    xc = xf - mu
    var = jnp.mean(xc * xc, axis=-1, keepdims=True)
    return (xc * lax.rsqrt(var + EPS)).astype(x.dtype) * g + b


def causal_dwconv(x, w, b):
    k = w.shape[0]
    y = lax.conv_general_dilated(
        x, w[:, None, :], window_strides=(1,), padding=[(k - 1, 0)],
        dimension_numbers=("NWC", "WIO", "NWC"), feature_group_count=x.shape[-1])
    return y + b


def rg_lru(x, w_a, b_a, w_x, b_x, lam):
    bsz, s, c = x.shape
    xh = x.reshape(bsz, s, LRU_HEADS, LRU_HD)
    r = jax.nn.sigmoid(jnp.einsum("bshi,hij->bshj", xh, w_a).reshape(bsz, s, c) + b_a)
    i = jax.nn.sigmoid(jnp.einsum("bshi,hij->bshj", xh, w_x).reshape(bsz, s, c) + b_x)
    log_a = -RG_C * r.astype(jnp.float32) * jax.nn.softplus(-lam.astype(jnp.float32))
    a = jnp.exp(log_a)
    mult = jnp.sqrt(-jnp.expm1(2.0 * log_a))
    u = mult * (i * x).astype(jnp.float32)

    def combine(left, right):
        a1, b1 = left
        a2, b2 = right
        return a1 * a2, a2 * b1 + b2

    _, h = lax.associative_scan(combine, (a, u), axis=1)
    return h.astype(x.dtype)


def setup_inputs(seed: int = 0) -> dict:
    key = jax.random.key(seed)
    ks = iter(jax.random.split(key, 40))
    f32 = jnp.float32

    def nrm(shape, scale):
        return jax.random.normal(next(ks), shape, f32) * scale

    def gain(shape):
        return 1.0 + 0.01 * jax.random.normal(next(ks), shape, f32)

    L = DEPTH
    x = jax.random.normal(next(ks), (BATCH, SEQ, D_MODEL), f32)
    mem = jax.random.normal(next(ks), (BATCH, N_MEM, D_MODEL), f32)
    a_c = jax.random.uniform(next(ks), (L, D_LRU), f32, 0.9, 0.999)
    sig = a_c ** (1.0 / RG_C)
    lam = jnp.log(sig) - jnp.log1p(-sig)
    return {
        "x": x,
        "mem": mem,
        "mix_norm_g": gain((L, D_MODEL)),
        "w_in": nrm((L, D_MODEL, 2 * D_MIX), D_MODEL ** -0.5),
        "lru_conv_w": nrm((L, LRU_CONV, D_LRU), LRU_CONV ** -0.5),
        "lru_conv_b": nrm((L, D_LRU), 0.01),
        "lru_w_a": nrm((L, LRU_HEADS, LRU_HD, LRU_HD), LRU_HD ** -0.5),
        "lru_b_a": nrm((L, D_LRU), 0.01),
        "lru_w_x": nrm((L, LRU_HEADS, LRU_HD, LRU_HD), LRU_HD ** -0.5),
        "lru_b_x": nrm((L, D_LRU), 0.01),
        "lru_lambda": lam,
        "conf_conv_w": nrm((L, CONF_CONV, D_CONF), CONF_CONV ** -0.5),
        "conf_conv_b": nrm((L, D_CONF), 0.01),
        "conf_ln_g": gain((L, D_CONF)),
        "conf_ln_b": nrm((L, D_CONF), 0.01),
        "w_out": nrm((L, D_MIX, D_MODEL), D_MIX ** -0.5),
        "xa_norm_g": gain((L, D_MODEL)),
        "mem_norm_g": gain((L, D_MODEL)),
        "w_q": nrm((L, D_MODEL, D_MODEL), D_MODEL ** -0.5),
        "w_kv": nrm((L, D_MODEL, 2 * D_MODEL), D_MODEL ** -0.5),
        "w_o": nrm((L, D_MODEL, D_MODEL), D_MODEL ** -0.5),
        "ffn_norm_g": gain((L, D_MODEL)),
        "w_up": nrm((L, D_MODEL, 2 * D_FF), D_MODEL ** -0.5),
        "ffn_conv_w": nrm((L, FFN_CONV, D_FF), FFN_CONV ** -0.5),
        "ffn_conv_b": nrm((L, D_FF), 0.01),
        "w_down": nrm((L, D_FF, D_MODEL), D_FF ** -0.5),
        "final_norm_g": gain((D_MODEL,)),
    }


def reference(x, mem, mix_norm_g, w_in, lru_conv_w, lru_conv_b, lru_w_a, lru_b_a,
              lru_w_x, lru_b_x, lru_lambda, conf_conv_w, conf_conv_b, conf_ln_g,
              conf_ln_b, w_out, xa_norm_g, mem_norm_g, w_q, w_kv, w_o, ffn_norm_g,
              w_up, ffn_conv_w, ffn_conv_b, w_down, final_norm_g):
    bsz, s, d = x.shape
    m_len = mem.shape[1]
    for l in range(DEPTH):
        h = rms_norm(x, mix_norm_g[l])
        z = h @ w_in[l]
        lru_x, lru_gate, conf_a, conf_b = jnp.split(
            z, [D_LRU, 2 * D_LRU, 2 * D_LRU + D_CONF], axis=-1)
        lru_x = causal_dwconv(lru_x, lru_conv_w[l], lru_conv_b[l])
        y_lru = rg_lru(lru_x, lru_w_a[l], lru_b_a[l], lru_w_x[l], lru_b_x[l],
                       lru_lambda[l]) * jax.nn.gelu(lru_gate, approximate=True)
        c = conf_a * jax.nn.sigmoid(conf_b)
        c = causal_dwconv(c, conf_conv_w[l], conf_conv_b[l])
        c = jax.nn.silu(layer_norm(c, conf_ln_g[l], conf_ln_b[l]))
        y = jnp.concatenate([y_lru, c], axis=-1) @ w_out[l]
        x = x + y

        h = rms_norm(x, xa_norm_g[l])
        m = rms_norm(mem, mem_norm_g[l])
        q = (h @ w_q[l]).reshape(bsz, s, XA_HEADS, XA_HD)
        kv = m @ w_kv[l]
        k, v = jnp.split(kv, 2, axis=-1)
        k = k.reshape(bsz, m_len, XA_HEADS, XA_HD)
        v = v.reshape(bsz, m_len, XA_HEADS, XA_HD)
        scores = jnp.einsum("bshd,bmhd->bhsm", q, k).astype(jnp.float32) * (XA_HD ** -0.5)
        p = jax.nn.softmax(scores, axis=-1).astype(v.dtype)
        o = jnp.einsum("bhsm,bmhd->bshd", p, v).reshape(bsz, s, d)
        x = x + o @ w_o[l]

        h = rms_norm(x, ffn_norm_g[l])
        gu = h @ w_up[l]
        g, u = jnp.split(gu, 2, axis=-1)
        g = causal_dwconv(g, ffn_conv_w[l], ffn_conv_b[l])
        x = x + (jax.nn.gelu(g, approximate=True) * u) @ w_down[l]
    return rms_norm(x, final_norm_g)
```

```python
import functools
import math

import jax
import jax.numpy as jnp
from jax import lax
from jax.experimental import pallas as pl
from jax.experimental.pallas import tpu as pltpu

D_MODEL = 1024
N_MEM = 256
D_LRU = 512
D_CONF = 512
LRU_HEADS = 8
LRU_HD = 64
LRU_CONV = 4
RG_C = 8.0
CONF_CONV = 31
XA_HEADS = 4
XA_HD = 256
D_FF = 3072
FFN_CONV = 3
EPS = 1e-6

SUBLANES = 8
HALF = 256
TM = 512
FF_CHUNK = 512
CONF_HALO = 32
VMEM_LIMIT = 56 * 1024 * 1024

_BF16 = jnp.bfloat16
_F32 = jnp.float32


def _rms(x, g):
    return x * lax.rsqrt(jnp.mean(x * x, axis=-1, keepdims=True) + EPS) * g


def _gelu_tanh(x):
    c = math.sqrt(2.0 / math.pi)
    return x * (0.5 * (1.0 + jnp.tanh(c * (x + 0.044715 * (x * x * x)))))


def _sigmoid(x):
    return 1.0 / (1.0 + jnp.exp(-x))


def _dot(a, b):
    return jnp.dot(a, b, preferred_element_type=_F32)


def _shift_rows_1(x):
    return pltpu.roll(x, 1, 0)


def _kv_kernel(mem_ref, g_ref, wkv_ref, k_ref, v_ref):
    m = _rms(mem_ref[...], g_ref[...]).astype(_BF16)
    kv = _dot(m, wkv_ref[...])
    k_ref[...] = (kv[:, :D_MODEL] * (XA_HD ** -0.5)).astype(_BF16)
    v_ref[...] = kv[:, D_MODEL:].astype(_BF16)


def _mix_kernel(x_ref, g_ref, win_ref, cw4_ref, cb4_ref, wgate_ref, ba_ref, bx_ref, lam_ref,
                cw31_ref, cb31_ref, lng_ref, lnb_ref, wout_ref, o_ref,
                xbuf, cbuf, hcar, abuf, ubuf, ycat):
    t = pl.program_id(1)

    @pl.when(t == 0)
    def _():
        xbuf[0:SUBLANES, :] = jnp.zeros((SUBLANES, D_LRU), _F32)
        cbuf[0:CONF_HALO, :] = jnp.zeros((CONF_HALO, D_CONF), _F32)
        hcar[...] = jnp.zeros((SUBLANES, D_LRU), _F32)

    @pl.when(t > 0)
    def _():
        xbuf[0:SUBLANES, :] = xbuf[TM:TM + SUBLANES, :]
        cbuf[0:CONF_HALO, :] = cbuf[TM:TM + CONF_HALO, :]

    x = x_ref[...]
    h = _rms(x, g_ref[...]).astype(_BF16)

    xbuf[SUBLANES:SUBLANES + TM, :] = _dot(h, win_ref[:, 0:D_LRU])
    ext = xbuf[...]
    acc = cw4_ref[0:1, :] * ext
    for k in range(1, LRU_CONV):
        acc = _shift_rows_1(acc) + cw4_ref[k:k + 1, :] * ext
    xc = acc[SUBLANES:, :] + cb4_ref[...]

    xcb = xc.astype(_BF16)
    ga, gx = [], []
    for j in range(D_LRU // HALF):
        gj = _dot(xcb[:, j * HALF:(j + 1) * HALF], wgate_ref[j])
        ga.append(gj[:, :HALF])
        gx.append(gj[:, HALF:])
    r = _sigmoid(jnp.concatenate(ga, axis=-1) + ba_ref[...])
    i = _sigmoid(jnp.concatenate(gx, axis=-1) + bx_ref[...])
    nlam = -lam_ref[...]
    softplus = jnp.maximum(nlam, 0.0) + jnp.log1p(jnp.exp(-jnp.abs(nlam)))
    log_a = (-RG_C * softplus) * r
    a = jnp.exp(log_a)
    u = jnp.sqrt(-jnp.tanh(log_a) * (a * a + 1.0)) * (i * xc)

    groups = TM // SUBLANES
    a3 = a.reshape(groups, SUBLANES, D_LRU)
    u3 = u.reshape(groups, SUBLANES, D_LRU)
    sub = lax.broadcasted_iota(jnp.int32, (groups, SUBLANES, D_LRU), 1)
    for k in (1, 2, 4):
        keep = sub >= k
        a_prev = jnp.where(keep, pltpu.roll(a3, k, 1), 1.0)
        u_prev = jnp.where(keep, pltpu.roll(u3, k, 1), 0.0)
        u3 = u3 + a3 * u_prev
        a3 = a3 * a_prev
    abuf[...] = a3.reshape(TM, D_LRU)
    ubuf[...] = u3.reshape(TM, D_LRU)
    carry = hcar[...]
    for gidx in range(groups):
        rows = pl.ds(gidx * SUBLANES, SUBLANES)
        hg = ubuf[rows, :] + abuf[rows, :] * carry
        ubuf[rows, :] = hg
        carry = jnp.broadcast_to(hg[SUBLANES - 1:SUBLANES, :], (SUBLANES, D_LRU))
    hcar[...] = carry
    gate = _dot(h, win_ref[:, D_LRU:2 * D_LRU])
    ycat[:, 0:D_LRU] = (ubuf[...] * _gelu_tanh(gate)).astype(_BF16)

    ca = _dot(h, win_ref[:, 2 * D_LRU:2 * D_LRU + D_CONF])
    cb = _dot(h, win_ref[:, 2 * D_LRU + D_CONF:])
    cbuf[CONF_HALO:CONF_HALO + TM, :] = ca * _sigmoid(cb)
    ext_rows = TM + SUBLANES
    acc = None
    for rr in range(SUBLANES - 1, -1, -1):
        part = None
        for q in range((CONF_CONV - 1 - rr) // SUBLANES + 1):
            d = SUBLANES * q + rr
            k = CONF_CONV - 1 - d
            start = CONF_HALO - SUBLANES - SUBLANES * q
            term = cw31_ref[k:k + 1, :] * cbuf[start:start + ext_rows, :]
            part = term if part is None else part + term
        acc = part if acc is None else _shift_rows_1(acc) + part
    c = acc[SUBLANES:, :] + cb31_ref[...]
    mu = jnp.mean(c, axis=-1, keepdims=True)
    cc = c - mu
    var = jnp.mean(cc * cc, axis=-1, keepdims=True)
    cn = cc * lax.rsqrt(var + EPS) * lng_ref[...] + lnb_ref[...]
    ycat[:, D_LRU:] = (cn * _sigmoid(cn)).astype(_BF16)

    o_ref[...] = x + _dot(ycat[...], wout_ref[...])


def _xattn_kernel(x_ref, g_ref, wq_ref, k_ref, v_ref, wo_ref, o_ref, obuf):
    x = x_ref[...]
    h = _rms(x, g_ref[...]).astype(_BF16)
    q = _dot(h, wq_ref[...]).astype(_BF16)
    for hd in range(XA_HEADS):
        cols = slice(hd * XA_HD, (hd + 1) * XA_HD)
        s = lax.dot_general(q[:, cols], k_ref[:, cols], (((1,), (1,)), ((), ())),
                            preferred_element_type=_F32)
        e = jnp.exp(s - jnp.max(s, axis=-1, keepdims=True))
        p = e * (1.0 / jnp.sum(e, axis=-1, keepdims=True))
        obuf[:, cols] = _dot(p.astype(_BF16), v_ref[:, cols]).astype(_BF16)
    o_ref[...] = x + _dot(obuf[...], wo_ref[...])


def _ffn_kernel(x_ref, g_ref, wup_ref, cw_ref, cb_ref, wdown_ref, gf_ref, o_ref, ghalo, actbuf):
    t = pl.program_id(1)

    @pl.when(t == 0)
    def _():
        ghalo[...] = jnp.zeros((SUBLANES, D_FF), _F32)

    x = x_ref[...]
    h = _rms(x, g_ref[...]).astype(_BF16)
    for c0 in range(0, D_FF, FF_CHUNK):
        cols = slice(c0, c0 + FF_CHUNK)
        g = _dot(h, wup_ref[:, cols])
        ext = jnp.concatenate([ghalo[:, cols], g], axis=0)
        ghalo[:, cols] = g[TM - SUBLANES:, :]
        acc = cw_ref[0:1, cols] * ext
        for k in range(1, FFN_CONV):
            acc = _shift_rows_1(acc) + cw_ref[k:k + 1, cols] * ext
        gc = acc[SUBLANES:, :] + cb_ref[:, cols]
        u = _dot(h, wup_ref[:, D_FF + c0:D_FF + c0 + FF_CHUNK])
        actbuf[:, cols] = (_gelu_tanh(gc) * u).astype(_BF16)
    y = x + _dot(actbuf[...], wdown_ref[...])
    o_ref[...] = _rms(y, gf_ref[...])


def _const_spec(shape):
    nd = len(shape)
    return pl.BlockSpec(shape, lambda b, t: (0,) * nd, pipeline_mode=pl.Buffered(1))


def _row_spec():
    return pl.BlockSpec((None, TM, D_MODEL), lambda b, t: (b, t, 0))


def _params():
    return pltpu.CompilerParams(dimension_semantics=("arbitrary", "arbitrary"),
                                vmem_limit_bytes=VMEM_LIMIT)


def _block_diag_gates(w_a, w_x):
    heads_per_half = HALF // LRU_HD
    eye = jnp.eye(heads_per_half, dtype=w_a.dtype)

    def bd(w):
        w = w.reshape(D_LRU // HALF, heads_per_half, LRU_HD, LRU_HD)
        return jnp.einsum("jhik,hg->jhigk", w, eye).reshape(D_LRU // HALF, HALF, HALF)

    return jnp.concatenate([bd(w_a), bd(w_x)], axis=-1)


def kernel(x, mem, mix_norm_g, w_in, lru_conv_w, lru_conv_b, lru_w_a, lru_b_a, lru_w_x, lru_b_x, lru_lambda, conf_conv_w, conf_conv_b, conf_ln_g, conf_ln_b, w_out, xa_norm_g, mem_norm_g, w_q, w_kv, w_o, ffn_norm_g, w_up, ffn_conv_w, ffn_conv_b, w_down, final_norm_g):
    bsz, seq, d = x.shape
    assert d == D_MODEL and seq % TM == 0 and mem.shape == (bsz, N_MEM, D_MODEL)
    grid = (bsz, seq // TM)
    row = lambda v: v.reshape(1, -1)
    x_shape = jax.ShapeDtypeStruct(x.shape, x.dtype)

    assert w_in.shape[0] == 1
    for l in range(1):
        k, v = pl.pallas_call(
            _kv_kernel,
            grid=(bsz,),
            in_specs=[pl.BlockSpec((None, N_MEM, D_MODEL), lambda b: (b, 0, 0)),
                      pl.BlockSpec((1, D_MODEL), lambda b: (0, 0)),
                      pl.BlockSpec((D_MODEL, 2 * D_MODEL), lambda b: (0, 0))],
            out_specs=[pl.BlockSpec((None, N_MEM, D_MODEL), lambda b: (b, 0, 0))] * 2,
            out_shape=[jax.ShapeDtypeStruct((bsz, N_MEM, D_MODEL), _BF16)] * 2,
            compiler_params=pltpu.CompilerParams(dimension_semantics=("arbitrary",),
                                                 vmem_limit_bytes=VMEM_LIMIT),
            name="kv_proj",
        )(mem, row(mem_norm_g[l]), w_kv[l].astype(_BF16))

        x = pl.pallas_call(
            _mix_kernel,
            grid=grid,
            in_specs=[_row_spec(),
                      _const_spec((1, D_MODEL)),
                      _const_spec((D_MODEL, 2 * (D_LRU + D_CONF))),
                      _const_spec((LRU_CONV, D_LRU)),
                      _const_spec((1, D_LRU)),
                      _const_spec((D_LRU // HALF, HALF, 2 * HALF)),
                      _const_spec((1, D_LRU)),
                      _const_spec((1, D_LRU)),
                      _const_spec((1, D_LRU)),
                      _const_spec((CONF_CONV, D_CONF)),
                      _const_spec((1, D_CONF)),
                      _const_spec((1, D_CONF)),
                      _const_spec((1, D_CONF)),
                      _const_spec((D_LRU + D_CONF, D_MODEL))],
            out_specs=_row_spec(),
            out_shape=x_shape,
            scratch_shapes=[pltpu.VMEM((SUBLANES + TM, D_LRU), _F32),
                            pltpu.VMEM((CONF_HALO + TM, D_CONF), _F32),
                            pltpu.VMEM((SUBLANES, D_LRU), _F32),
                            pltpu.VMEM((TM, D_LRU), _F32),
                            pltpu.VMEM((TM, D_LRU), _F32),
                            pltpu.VMEM((TM, D_LRU + D_CONF), _BF16)],
            compiler_params=_params(),
            name="mix",
        )(x, row(mix_norm_g[l]), w_in[l].astype(_BF16), lru_conv_w[l], row(lru_conv_b[l]),
          _block_diag_gates(lru_w_a[l], lru_w_x[l]).astype(_BF16), row(lru_b_a[l]), row(lru_b_x[l]),
          row(lru_lambda[l]), conf_conv_w[l], row(conf_conv_b[l]), row(conf_ln_g[l]), row(conf_ln_b[l]),
          w_out[l].astype(_BF16))

        kv_spec = pl.BlockSpec((None, N_MEM, D_MODEL), lambda b, t: (b, 0, 0))
        x = pl.pallas_call(
            _xattn_kernel,
            grid=grid,
            in_specs=[_row_spec(),
                      _const_spec((1, D_MODEL)),
                      _const_spec((D_MODEL, D_MODEL)),
                      kv_spec, kv_spec,
                      _const_spec((D_MODEL, D_MODEL))],
            out_specs=_row_spec(),
            out_shape=x_shape,
            scratch_shapes=[pltpu.VMEM((TM, D_MODEL), _BF16)],
            compiler_params=_params(),
            name="xattn",
        )(x, row(xa_norm_g[l]), w_q[l].astype(_BF16), k, v, w_o[l].astype(_BF16))

        x = pl.pallas_call(
            _ffn_kernel,
            grid=grid,
            in_specs=[_row_spec(),
                      _const_spec((1, D_MODEL)),
                      _const_spec((D_MODEL, 2 * D_FF)),
                      _const_spec((FFN_CONV, D_FF)),
                      _const_spec((1, D_FF)),
                      _const_spec((D_FF, D_MODEL)),
                      _const_spec((1, D_MODEL))],
            out_specs=_row_spec(),
            out_shape=x_shape,
            scratch_shapes=[pltpu.VMEM((SUBLANES, D_FF), _F32),
                            pltpu.VMEM((TM, D_FF), _BF16)],
            compiler_params=_params(),
            name="ffn",
        )(x, row(ffn_norm_g[l]), w_up[l].astype(_BF16), ffn_conv_w[l], row(ffn_conv_b[l]),
          w_down[l].astype(_BF16), row(final_norm_g))
    return x
```

```python
import functools
import math

import jax
import jax.numpy as jnp
from jax import lax
from jax.experimental import pallas as pl
from jax.experimental.pallas import tpu as pltpu

D_MODEL = 1024
N_MEM = 256
D_LRU = 512
D_CONF = 512
LRU_HD = 64
LRU_CONV = 4
RG_C = 8.0
CONF_CONV = 31
XA_HEADS = 4
XA_HD = 256
D_FF = 3072
FFN_CONV = 3
EPS = 1e-6

SUBLANES = 8
HALF = 256
TM = 512
FF_CHUNK = 512
CONF_HALO = 32
VMEM_LIMIT_KV = 32 * 1024 * 1024
VMEM_LIMIT_LAYER = 62 * 1024 * 1024

_BF16 = jnp.bfloat16
_F32 = jnp.float32
_DONE = object()


def _rms(x, g):
    return x * lax.rsqrt(jnp.mean(x * x, axis=-1, keepdims=True) + EPS) * g


def _gelu_tanh(x):
    c = math.sqrt(2.0 / math.pi)
    return x * (0.5 * (1.0 + jnp.tanh(c * (x + 0.044715 * (x * x * x)))))


def _sigmoid(x):
    return 1.0 / (1.0 + jnp.exp(-x))


def _dot(a, b):
    return jnp.dot(a, b, preferred_element_type=_F32)


def _shift_rows_1(x):
    return pltpu.roll(x, 1, 0)


def _kv_kernel(mem_ref, g_ref, wkv_ref, k_ref, v_ref):
    m = _rms(mem_ref[...], g_ref[...]).astype(_BF16)
    kv = _dot(m, wkv_ref[...])
    k_ref[...] = (kv[:, :D_MODEL] * (XA_HD ** -0.5)).astype(_BF16)
    v_ref[...] = kv[:, D_MODEL:].astype(_BF16)


def _mix_carry_in(first_tile, xbuf, cbuf, hcar):
    @pl.when(first_tile)
    def _():
        xbuf[0:SUBLANES, :] = jnp.zeros((SUBLANES, D_LRU), _F32)
        cbuf[0:CONF_HALO, :] = jnp.zeros((CONF_HALO, D_CONF), _F32)
        hcar[...] = jnp.zeros((SUBLANES, D_LRU), _F32)

    @pl.when(jnp.logical_not(first_tile))
    def _():
        xbuf[0:SUBLANES, :] = xbuf[TM:TM + SUBLANES, :]
        cbuf[0:CONF_HALO, :] = cbuf[TM:TM + CONF_HALO, :]


def _mix_project(x_ref, g_ref, win_ref, xbuf, cbuf, gatebuf):
    h = _rms(x_ref[...], g_ref[...]).astype(_BF16)
    xbuf[SUBLANES:SUBLANES + TM, :] = _dot(h, win_ref[:, 0:D_LRU])
    for c0 in range(0, D_CONF, HALF):
        yield
        ca = _dot(h, win_ref[:, 2 * D_LRU + c0:2 * D_LRU + c0 + HALF])
        cb = _dot(h, win_ref[:, 2 * D_LRU + D_CONF + c0:2 * D_LRU + D_CONF + c0 + HALF])
        cbuf[CONF_HALO:CONF_HALO + TM, c0:c0 + HALF] = ca * _sigmoid(cb)
    yield
    gatebuf[...] = _dot(h, win_ref[:, D_LRU:2 * D_LRU])


def _lru_branch(cw4_ref, cb4_ref, wgate_ref, ba_ref, bx_ref, lam_ref, xbuf, gatebuf, hcar, abuf, ubuf, ycat):
    ext = xbuf[...]
    acc = cw4_ref[0:1, :] * ext
    for k in range(1, LRU_CONV):
        acc = _shift_rows_1(acc) + cw4_ref[k:k + 1, :] * ext
    xc = acc[SUBLANES:, :] + cb4_ref[...]

    xcb = xc.astype(_BF16)
    ga, gx = [], []
    for j in range(D_LRU // HALF):
        gj = _dot(xcb[:, j * HALF:(j + 1) * HALF], wgate_ref[j])
        ga.append(gj[:, :HALF])
        gx.append(gj[:, HALF:])
    yield
    r = _sigmoid(jnp.concatenate(ga, axis=-1) + ba_ref[...])
    i = _sigmoid(jnp.concatenate(gx, axis=-1) + bx_ref[...])
    nlam = -lam_ref[...]
    softplus = jnp.maximum(nlam, 0.0) + jnp.log1p(jnp.exp(-jnp.abs(nlam)))
    log_a = (-RG_C * softplus) * r
    a = jnp.exp(log_a)
    u = jnp.sqrt(-jnp.tanh(log_a) * (a * a + 1.0)) * (i * xc)
    yield

    groups = TM // SUBLANES
    a3 = a.reshape(groups, SUBLANES, D_LRU)
    u3 = u.reshape(groups, SUBLANES, D_LRU)
    sub = lax.broadcasted_iota(jnp.int32, (groups, SUBLANES, D_LRU), 1)
    for k in (1, 2, 4):
        keep = sub >= k
        a_prev = jnp.where(keep, pltpu.roll(a3, k, 1), 1.0)
        u_prev = jnp.where(keep, pltpu.roll(u3, k, 1), 0.0)
        u3 = u3 + a3 * u_prev
        a3 = a3 * a_prev
    abuf[...] = a3.reshape(TM, D_LRU)
    ubuf[...] = u3.reshape(TM, D_LRU)
    carry = hcar[...]
    for gidx in range(groups):
        rows = pl.ds(gidx * SUBLANES, SUBLANES)
        hg = ubuf[rows, :] + abuf[rows, :] * carry
        ubuf[rows, :] = hg
        carry = jnp.broadcast_to(hg[SUBLANES - 1:SUBLANES, :], (SUBLANES, D_LRU))
    hcar[...] = carry
    ycat[:, 0:D_LRU] = (ubuf[...] * _gelu_tanh(gatebuf[...])).astype(_BF16)


def _conf_conv31(cols, cw31_ref, cb31_ref, cbuf, cobuf):
    ext_rows = TM + SUBLANES
    acc = None
    for rr in range(SUBLANES - 1, -1, -1):
        part = None
        for q in range((CONF_CONV - 1 - rr) // SUBLANES + 1):
            d = SUBLANES * q + rr
            k = CONF_CONV - 1 - d
            start = CONF_HALO - SUBLANES - SUBLANES * q
            term = cw31_ref[k:k + 1, cols] * cbuf[start:start + ext_rows, cols]
            part = term if part is None else part + term
        acc = part if acc is None else _shift_rows_1(acc) + part
    cobuf[:, cols] = acc[SUBLANES:, :] + cb31_ref[:, cols]


def _conf_norm(cobuf, lng_ref, lnb_ref, ycat):
    c = cobuf[...]
    mu = jnp.mean(c, axis=-1, keepdims=True)
    cc = c - mu
    var = jnp.mean(cc * cc, axis=-1, keepdims=True)
    cn = cc * lax.rsqrt(var + EPS) * lng_ref[...] + lnb_ref[...]
    ycat[:, D_LRU:] = (cn * _sigmoid(cn)).astype(_BF16)


def _mix_output(x_ref, ycat, wout_ref, o_ref):
    o_ref[...] = x_ref[...] + _dot(ycat[...], wout_ref[...])


def _xattn(x_ref, g_ref, wq_ref, k_ref, v_ref, wo_ref, o_ref, obuf):
    x = x_ref[...]
    h = _rms(x, g_ref[...]).astype(_BF16)
    q = _dot(h, wq_ref[...]).astype(_BF16)
    yield
    heads = [slice(hd * XA_HD, (hd + 1) * XA_HD) for hd in range(XA_HEADS)]
    scores = [lax.dot_general(q[:, cols], k_ref[:, cols], (((1,), (1,)), ((), ())), preferred_element_type=_F32)
              for cols in heads]
    yield
    for cols, s in zip(heads, scores):
        e = jnp.exp(s - jnp.max(s, axis=-1, keepdims=True))
        p = e * (1.0 / jnp.sum(e, axis=-1, keepdims=True))
        obuf[:, cols] = _dot(p.astype(_BF16), v_ref[:, cols]).astype(_BF16)
    yield
    o_ref[...] = x + _dot(obuf[...], wo_ref[...])


def _ffn(x_ref, g_ref, wup_ref, cw_ref, cb_ref, wdown_ref, gf_ref, o_ref, ghalo, actbuf):
    x = x_ref[...]
    h = _rms(x, g_ref[...]).astype(_BF16)
    for c0 in range(0, D_FF, FF_CHUNK):
        cols = slice(c0, c0 + FF_CHUNK)
        g = _dot(h, wup_ref[:, cols])
        ext = jnp.concatenate([ghalo[:, cols], g], axis=0)
        ghalo[:, cols] = g[TM - SUBLANES:, :]
        acc = cw_ref[0:1, cols] * ext
        for k in range(1, FFN_CONV):
            acc = _shift_rows_1(acc) + cw_ref[k:k + 1, cols] * ext
        gc = acc[SUBLANES:, :] + cb_ref[:, cols]
        u = _dot(h, wup_ref[:, D_FF + c0:D_FF + c0 + FF_CHUNK])
        actbuf[:, cols] = (_gelu_tanh(gc) * u).astype(_BF16)
        yield
    y = x + _dot(actbuf[...], wdown_ref[...])
    o_ref[...] = _rms(y, gf_ref[...])


def _layer_kernel(tiles_per_seq,
                  x_ref, k_ref, v_ref,
                  g_mix, win_ref, cw4_ref, cb4_ref, wgate_ref, ba_ref, bx_ref, lam_ref,
                  cw31_ref, cb31_ref, lng_ref, lnb_ref, wout_ref,
                  g_xa, wq_ref, wo_ref,
                  g_ffn, wup_ref, cwf_ref, cbf_ref, wdown_ref, g_final,
                  o_ref,
                  x1buf, x2buf, xbuf, cbuf, cobuf, gatebuf, hcar, abuf, ubuf, ycat, obuf, ghalo, actbuf):
    s = pl.program_id(0)
    n_tiles = pl.num_programs(0) - 1
    mix_tile = jnp.minimum(s, n_tiles - 1)
    out_tile = jnp.maximum(s - 1, 0)

    @pl.when(s == 0)
    def _():
        x1buf[...] = jnp.zeros((TM, D_MODEL), _F32)

    @pl.when(out_tile % tiles_per_seq == 0)
    def _():
        ghalo[...] = jnp.zeros((SUBLANES, D_FF), _F32)

    _mix_carry_in(mix_tile % tiles_per_seq == 0, xbuf, cbuf, hcar)

    mp = _mix_project(x_ref, g_mix, win_ref, xbuf, cbuf, gatebuf)
    lb = _lru_branch(cw4_ref, cb4_ref, wgate_ref, ba_ref, bx_ref, lam_ref, xbuf, gatebuf, hcar, abuf, ubuf, ycat)
    xa = _xattn(x1buf, g_xa, wq_ref, k_ref, v_ref, wo_ref, x2buf, obuf)
    ff = _ffn(x2buf, g_ffn, wup_ref, cwf_ref, cbf_ref, wdown_ref, g_final, o_ref, ghalo, actbuf)
    conv = [functools.partial(_conf_conv31, slice(c0, c0 + HALF), cw31_ref, cb31_ref, cbuf, cobuf)
            for c0 in range(0, D_CONF, HALF)]
    norm = functools.partial(_conf_norm, cobuf, lng_ref, lnb_ref, ycat)
    out = functools.partial(_mix_output, x_ref, ycat, wout_ref, x1buf)
    schedule = [mp, mp, mp,
                xa,
                lb,
                xa, mp, xa,
                xa,
                conv[0], conv[1], lb, lb, norm,
                ff, ff, ff, ff, ff, ff,
                ff, out]
    stages = (mp, lb, xa, ff)
    live = {id(g): True for g in stages}
    for piece in schedule:
        if any(piece is g for g in stages):
            live[id(piece)] = next(piece, _DONE) is not _DONE
        else:
            piece()
    assert not any(live.values()), "schedule must exhaust every stage"


def _const_spec(shape):
    nd = len(shape)
    return pl.BlockSpec(shape, lambda s: (0,) * nd, pipeline_mode=pl.Buffered(1))


def _block_diag_gates(w_a, w_x):
    heads_per_half = HALF // LRU_HD
    eye = jnp.eye(heads_per_half, dtype=w_a.dtype)

    def bd(w):
        w = w.reshape(D_LRU // HALF, heads_per_half, LRU_HD, LRU_HD)
        return jnp.einsum("jhik,hg->jhigk", w, eye).reshape(D_LRU // HALF, HALF, HALF)

    return jnp.concatenate([bd(w_a), bd(w_x)], axis=-1)


def kernel(x, mem, mix_norm_g, w_in, lru_conv_w, lru_conv_b, lru_w_a, lru_b_a, lru_w_x, lru_b_x, lru_lambda, conf_conv_w, conf_conv_b, conf_ln_g, conf_ln_b, w_out, xa_norm_g, mem_norm_g, w_q, w_kv, w_o, ffn_norm_g, w_up, ffn_conv_w, ffn_conv_b, w_down, final_norm_g):
    bsz, seq, d = x.shape
    assert d == D_MODEL and seq % TM == 0 and mem.shape == (bsz, N_MEM, D_MODEL)
    assert w_in.shape[0] == 1
    tiles_per_seq = seq // TM
    n_tiles = bsz * tiles_per_seq
    row = lambda v: v.reshape(1, -1)
    bf = lambda w: w[0].astype(_BF16)

    k, v = pl.pallas_call(
        _kv_kernel,
        grid=(bsz,),
        in_specs=[pl.BlockSpec((None, N_MEM, D_MODEL), lambda b: (b, 0, 0)),
                  pl.BlockSpec((1, D_MODEL), lambda b: (0, 0)),
                  pl.BlockSpec((D_MODEL, 2 * D_MODEL), lambda b: (0, 0))],
        out_specs=[pl.BlockSpec((None, N_MEM, D_MODEL), lambda b: (b, 0, 0))] * 2,
        out_shape=[jax.ShapeDtypeStruct((bsz, N_MEM, D_MODEL), _BF16)] * 2,
        compiler_params=pltpu.CompilerParams(dimension_semantics=("arbitrary",),
                                             vmem_limit_bytes=VMEM_LIMIT_KV),
        name="kv_proj",
    )(mem, row(mem_norm_g[0]), bf(w_kv))

    def mix_rows(s):
        j = jnp.minimum(s, n_tiles - 1)
        return (j // tiles_per_seq, j % tiles_per_seq, 0)

    def out_rows(s):
        j = jnp.maximum(s - 1, 0)
        return (j // tiles_per_seq, j % tiles_per_seq, 0)

    kv_spec = pl.BlockSpec((None, N_MEM, D_MODEL), lambda s: (jnp.maximum(s - 1, 0) // tiles_per_seq, 0, 0))
    f32_scratch = lambda rows, cols: pltpu.VMEM((rows, cols), _F32)
    bf16_scratch = lambda rows, cols: pltpu.VMEM((rows, cols), _BF16)
    return pl.pallas_call(
        functools.partial(_layer_kernel, tiles_per_seq),
        grid=(n_tiles + 1,),
        in_specs=[pl.BlockSpec((None, TM, D_MODEL), mix_rows), kv_spec, kv_spec,
                  _const_spec((1, D_MODEL)),
                  _const_spec((D_MODEL, 2 * (D_LRU + D_CONF))),
                  _const_spec((LRU_CONV, D_LRU)),
                  _const_spec((1, D_LRU)),
                  _const_spec((D_LRU // HALF, HALF, 2 * HALF)),
                  _const_spec((1, D_LRU)),
                  _const_spec((1, D_LRU)),
                  _const_spec((1, D_LRU)),
                  _const_spec((CONF_CONV, D_CONF)),
                  _const_spec((1, D_CONF)),
                  _const_spec((1, D_CONF)),
                  _const_spec((1, D_CONF)),
                  _const_spec((D_LRU + D_CONF, D_MODEL)),
                  _const_spec((1, D_MODEL)),
                  _const_spec((D_MODEL, D_MODEL)),
                  _const_spec((D_MODEL, D_MODEL)),
                  _const_spec((1, D_MODEL)),
                  _const_spec((D_MODEL, 2 * D_FF)),
                  _const_spec((FFN_CONV, D_FF)),
                  _const_spec((1, D_FF)),
                  _const_spec((D_FF, D_MODEL)),
                  _const_spec((1, D_MODEL))],
        out_specs=pl.BlockSpec((None, TM, D_MODEL), out_rows),
        out_shape=jax.ShapeDtypeStruct(x.shape, x.dtype),
        scratch_shapes=[f32_scratch(TM, D_MODEL),
                        f32_scratch(TM, D_MODEL),
                        f32_scratch(SUBLANES + TM, D_LRU),
                        f32_scratch(CONF_HALO + TM, D_CONF),
                        f32_scratch(TM, D_CONF),
                        f32_scratch(TM, D_LRU),
                        f32_scratch(SUBLANES, D_LRU),
                        f32_scratch(TM, D_LRU),
                        f32_scratch(TM, D_LRU),
                        bf16_scratch(TM, D_LRU + D_CONF),
                        bf16_scratch(TM, D_MODEL),
                        f32_scratch(SUBLANES, D_FF),
                        bf16_scratch(TM, D_FF)],
        compiler_params=pltpu.CompilerParams(dimension_semantics=("arbitrary",),
                                             vmem_limit_bytes=VMEM_LIMIT_LAYER),
        name="layer",
    )(x, k, v,
      row(mix_norm_g[0]), bf(w_in), lru_conv_w[0], row(lru_conv_b[0]),
      _block_diag_gates(lru_w_a[0], lru_w_x[0]).astype(_BF16), row(lru_b_a[0]), row(lru_b_x[0]),
      row(lru_lambda[0]), conf_conv_w[0], row(conf_conv_b[0]), row(conf_ln_g[0]), row(conf_ln_b[0]), bf(w_out),
      row(xa_norm_g[0]), bf(w_q), bf(w_o),
      row(ffn_norm_g[0]), bf(w_up), ffn_conv_w[0], row(ffn_conv_b[0]), bf(w_down), row(final_norm_g))
```

```python
import functools
import math

import jax
import jax.numpy as jnp
from jax import lax
from jax.experimental import pallas as pl
from jax.experimental.pallas import tpu as pltpu

D_MODEL = 1024
N_MEM = 256
D_LRU = 512
D_CONF = 512
LRU_HD = 64
LRU_CONV = 4
RG_C = 8.0
CONF_CONV = 31
XA_HEADS = 4
XA_HD = 256
D_FF = 3072
FFN_CONV = 3
EPS = 1e-6

SUBLANES = 8
HALF = 256
TM = 512
MIX_ROWS = TM
FF_CHUNK = 512
CONF_HALO = 32
VMEM_LIMIT_KV = 32 * 1024 * 1024
VMEM_LIMIT_LAYER = 62 * 1024 * 1024

_BF16 = jnp.bfloat16
_F32 = jnp.float32
_DONE = object()


def _rms(x, g):
    return x * lax.rsqrt(jnp.mean(x * x, axis=-1, keepdims=True) + EPS) * g


_LOG2E = math.log2(math.e)


def _gelu_tanh_times(x, y):
    c = math.sqrt(2.0 / math.pi)
    k1 = -2.0 * c * _LOG2E
    k3 = k1 * 0.044715
    e = jnp.exp2(x * (k1 + k3 * (x * x)))
    return (x * y) / (1.0 + e)


def _sigmoid(x):
    return 1.0 / (1.0 + jnp.exp2(x * (-_LOG2E)))


def _dot(a, b):
    return jnp.dot(a, b, preferred_element_type=_F32)


def _shift_rows_1(x):
    return pltpu.roll(x, 1, 0)


def _kv_kernel(mem_ref, g_ref, wkv_ref, k_ref, v_ref):
    m = _rms(mem_ref[...], g_ref[...]).astype(_BF16)
    kv = _dot(m, wkv_ref[...])
    k_ref[...] = (kv[:, :D_MODEL] * (XA_HD ** -0.5)).astype(_BF16)
    v_ref[...] = kv[:, D_MODEL:].astype(_BF16)


def _mix_carry_in(first_tile, xbuf, cbuf, hcar):
    @pl.when(first_tile)
    def _():
        xbuf[0:SUBLANES, :] = jnp.zeros((SUBLANES, D_LRU), _F32)
        cbuf[0:CONF_HALO, :] = jnp.zeros((CONF_HALO, D_CONF), _F32)
        hcar[...] = jnp.zeros((SUBLANES, D_LRU), _F32)

    @pl.when(jnp.logical_not(first_tile))
    def _():
        xbuf[0:SUBLANES, :] = xbuf[TM:TM + SUBLANES, :]
        cbuf[0:CONF_HALO, :] = cbuf[TM:TM + CONF_HALO, :]


def _conv4(r0, cw4_ref, cb4_ref, xbuf):
    ext = xbuf[r0:r0 + MIX_ROWS + SUBLANES, :]
    acc = cw4_ref[0:1, :] * ext
    for k in range(1, LRU_CONV):
        acc = _shift_rows_1(acc) + cw4_ref[k:k + 1, :] * ext
    return acc[SUBLANES:, :] + cb4_ref[...]


def _conv31(r0, cw31_ref, cb31_ref, cbuf):
    ext_rows = MIX_ROWS + SUBLANES
    win = cbuf[r0:r0 + CONF_HALO + MIX_ROWS, :]
    sources = (win, pltpu.roll(win, 4, 0))
    parts = [[None, None], [None, None]]
    for m in range(2):
        for r in range(2):
            for q in range(CONF_HALO // SUBLANES):
                for s in range(2):
                    d = 8 * q + 4 * s + 2 * m + r
                    if d >= CONF_CONV:
                        continue
                    k = CONF_CONV - 1 - d
                    start = CONF_HALO - SUBLANES - SUBLANES * q
                    term = cw31_ref[k:k + 1, :] * sources[s][start:start + ext_rows, :]
                    parts[m][r] = term if parts[m][r] is None else parts[m][r] + term
    even = parts[0][0] + _shift_rows_1(parts[0][1])
    odd = parts[1][0] + _shift_rows_1(parts[1][1])
    acc = even + pltpu.roll(odd, 2, 0)
    return acc[SUBLANES:, :] + cb31_ref[...]


def _recurrence(a, u, carry, abuf, ubuf):
    groups = MIX_ROWS // SUBLANES
    a3 = a.reshape(groups, SUBLANES, D_LRU)
    u3 = u.reshape(groups, SUBLANES, D_LRU)
    sub = lax.broadcasted_iota(jnp.int32, (groups, SUBLANES, D_LRU), 1)
    for k in (1, 2, 4):
        keep = sub >= k
        a_prev = jnp.where(keep, pltpu.roll(a3, k, 1), 1.0)
        u_prev = jnp.where(keep, pltpu.roll(u3, k, 1), 0.0)
        u3 = u3 + a3 * u_prev
        a3 = a3 * a_prev
    abuf[...] = a3.reshape(MIX_ROWS, D_LRU)
    ubuf[...] = u3.reshape(MIX_ROWS, D_LRU)
    for gidx in range(groups):
        rows = pl.ds(gidx * SUBLANES, SUBLANES)
        hg = ubuf[rows, :] + abuf[rows, :] * carry
        ubuf[rows, :] = hg
        carry = jnp.broadcast_to(hg[SUBLANES - 1:SUBLANES, :], (SUBLANES, D_LRU))
    return ubuf[...], carry


def _mix_block(r0, state, x_ref, g_ref, win_ref, cw4_ref, cb4_ref, wgate_ref, ba_ref, bx_ref, lam_ref,
               cw31_ref, cb31_ref, lng_ref, lnb_ref, wout_ref, o_ref,
               xbuf, cbuf, cbbuf, gatebuf, hcar, abuf, ubuf, ycat):
    rows = slice(r0, r0 + MIX_ROWS)
    h = _rms(x_ref[rows, :], g_ref[...]).astype(_BF16)
    z = _dot(h, win_ref[:, 0:2 * D_LRU])
    xbuf[SUBLANES + r0:SUBLANES + r0 + MIX_ROWS, :] = z[:, 0:D_LRU]
    gatebuf[rows, :] = z[:, D_LRU:]
    yield

    xc = _conv4(r0, cw4_ref, cb4_ref, xbuf)
    xcb = xc.astype(_BF16)
    for j in range(D_LRU // HALF):
        gj = _dot(xcb[:, j * HALF:(j + 1) * HALF], wgate_ref[j])
        abuf[:, j * HALF:(j + 1) * HALF] = gj[:, :HALF]
        ubuf[:, j * HALF:(j + 1) * HALF] = gj[:, HALF:]
    zc = _dot(h, win_ref[:, 2 * D_LRU:])
    conv_rows = slice(CONF_HALO + r0, CONF_HALO + r0 + MIX_ROWS)
    cbuf[conv_rows, :] = zc[:, 0:D_CONF]
    cbbuf[rows, :] = zc[:, D_CONF:]

    cbuf[conv_rows, :] = cbuf[conv_rows, :] * _sigmoid(cbbuf[rows, :])
    r = _sigmoid(abuf[...] + ba_ref[...])
    i = _sigmoid(ubuf[...] + bx_ref[...])
    nlam = -lam_ref[...]
    softplus = jnp.maximum(nlam, 0.0) + jnp.log1p(jnp.exp(-jnp.abs(nlam)))
    log_a = (-RG_C * softplus) * r
    a = jnp.exp(log_a)
    u = jnp.sqrt(-jnp.tanh(log_a) * (a * a + 1.0)) * (i * xc)
    assert state["next_row"] == r0, "row blocks must run their second piece in order"
    carry = hcar[...] if r0 == 0 else state["carry"]
    hseq, carry = _recurrence(a, u, carry, abuf, ubuf)
    state.update(next_row=r0 + MIX_ROWS, carry=carry)
    if r0 + MIX_ROWS == TM:
        hcar[...] = carry
    ycat[rows, 0:D_LRU] = _gelu_tanh_times(gatebuf[rows, :], hseq).astype(_BF16)

    c = _conv31(r0, cw31_ref, cb31_ref, cbuf)
    mu = jnp.mean(c, axis=-1, keepdims=True)
    cc = c - mu
    var = jnp.mean(cc * cc, axis=-1, keepdims=True)
    cn = cc * lax.rsqrt(var + EPS) * lng_ref[...] + lnb_ref[...]
    ycat[rows, D_LRU:] = (cn * _sigmoid(cn)).astype(_BF16)
    yield
    o_ref[rows, :] = x_ref[rows, :] + _dot(ycat[rows, :], wout_ref[...])


def _xattn(x_ref, g_ref, wq_ref, k_ref, v_ref, wo_ref, o_ref, obuf):
    x = x_ref[...]
    h = _rms(x, g_ref[...]).astype(_BF16)
    q = _dot(h, wq_ref[...]).astype(_BF16)
    yield
    heads = [slice(hd * XA_HD, (hd + 1) * XA_HD) for hd in range(XA_HEADS)]
    scores = [lax.dot_general(q[:, cols], k_ref[:, cols], (((1,), (1,)), ((), ())), preferred_element_type=_F32)
              for cols in heads]
    yield
    for cols, s in zip(heads, scores):
        e = jnp.exp(s - jnp.max(s, axis=-1, keepdims=True))
        p = e * (1.0 / jnp.sum(e, axis=-1, keepdims=True))
        obuf[:, cols] = _dot(p.astype(_BF16), v_ref[:, cols]).astype(_BF16)
    yield
    o_ref[...] = x + _dot(obuf[...], wo_ref[...])


def _ffn(x_ref, g_ref, wup_ref, cw_ref, cb_ref, wdown_ref, gf_ref, o_ref, ghalo, gbuf, ubuf2, actbuf):
    x = x_ref[...]
    h = _rms(x, g_ref[...]).astype(_BF16)
    for idx, c0 in enumerate(range(0, D_FF, FF_CHUNK)):
        cols = slice(c0, c0 + FF_CHUNK)
        slot = idx % 2
        gbuf[slot, 0:SUBLANES, :] = ghalo[:, cols]
        gbuf[slot, SUBLANES:SUBLANES + TM, :] = _dot(h, wup_ref[:, cols])
        ubuf2[slot] = _dot(h, wup_ref[:, D_FF + c0:D_FF + c0 + FF_CHUNK])
        ghalo[:, cols] = gbuf[slot, TM:TM + SUBLANES, :]
        ext = gbuf[slot]
        acc = cw_ref[0:1, cols] * ext
        for k in range(1, FFN_CONV):
            acc = _shift_rows_1(acc) + cw_ref[k:k + 1, cols] * ext
        gc = acc[SUBLANES:, :] + cb_ref[:, cols]
        actbuf[:, cols] = _gelu_tanh_times(gc, ubuf2[slot]).astype(_BF16)
        yield
    y = x + _dot(actbuf[...], wdown_ref[...])
    o_ref[...] = _rms(y, gf_ref[...])


def _layer_kernel(tiles_per_seq,
                  x_ref, k_ref, v_ref,
                  g_mix, win_ref, cw4_ref, cb4_ref, wgate_ref, ba_ref, bx_ref, lam_ref,
                  cw31_ref, cb31_ref, lng_ref, lnb_ref, wout_ref,
                  g_xa, wq_ref, wo_ref,
                  g_ffn, wup_ref, cwf_ref, cbf_ref, wdown_ref, g_final,
                  o_ref,
                  x1buf, x2buf, xbuf, cbuf, cbbuf, gatebuf, hcar, abuf, ubuf, ycat, obuf, ghalo, gbuf, ubuf2, actbuf):
    s = pl.program_id(0)
    n_tiles = pl.num_programs(0) - 1
    mix_tile = jnp.minimum(s, n_tiles - 1)
    out_tile = jnp.maximum(s - 1, 0)

    @pl.when(s == 0)
    def _():
        x1buf[...] = jnp.zeros((TM, D_MODEL), _F32)

    @pl.when(out_tile % tiles_per_seq == 0)
    def _():
        ghalo[...] = jnp.zeros((SUBLANES, D_FF), _F32)

    _mix_carry_in(mix_tile % tiles_per_seq == 0, xbuf, cbuf, hcar)

    state = {"next_row": 0, "carry": None}
    (m0,) = [
        _mix_block(r0, state, x_ref, g_mix, win_ref, cw4_ref, cb4_ref, wgate_ref, ba_ref, bx_ref, lam_ref,
                   cw31_ref, cb31_ref, lng_ref, lnb_ref, wout_ref, x1buf,
                   xbuf, cbuf, cbbuf, gatebuf, hcar, abuf, ubuf, ycat)
        for r0 in range(0, TM, MIX_ROWS)]
    xa = _xattn(x1buf, g_xa, wq_ref, k_ref, v_ref, wo_ref, x2buf, obuf)
    ff = _ffn(x2buf, g_ffn, wup_ref, cwf_ref, cbf_ref, wdown_ref, g_final, o_ref, ghalo, gbuf, ubuf2, actbuf)
    schedule = [m0, xa,
                xa, m0, xa,
                xa,
                ff, ff, ff, ff, ff, ff,
                ff, m0]
    stages = (m0, xa, ff)
    live = {id(g): True for g in stages}
    for g in schedule:
        live[id(g)] = next(g, _DONE) is not _DONE
    assert not any(live.values()), "schedule must exhaust every stage"


def _const_spec(shape):
    nd = len(shape)
    return pl.BlockSpec(shape, lambda s: (0,) * nd, pipeline_mode=pl.Buffered(1))


def _block_diag_gates(w_a, w_x):
    heads_per_half = HALF // LRU_HD
    eye = jnp.eye(heads_per_half, dtype=w_a.dtype)

    def bd(w):
        w = w.reshape(D_LRU // HALF, heads_per_half, LRU_HD, LRU_HD)
        return jnp.einsum("jhik,hg->jhigk", w, eye).reshape(D_LRU // HALF, HALF, HALF)

    return jnp.concatenate([bd(w_a), bd(w_x)], axis=-1)


def kernel(x, mem, mix_norm_g, w_in, lru_conv_w, lru_conv_b, lru_w_a, lru_b_a, lru_w_x, lru_b_x, lru_lambda, conf_conv_w, conf_conv_b, conf_ln_g, conf_ln_b, w_out, xa_norm_g, mem_norm_g, w_q, w_kv, w_o, ffn_norm_g, w_up, ffn_conv_w, ffn_conv_b, w_down, final_norm_g):
    bsz, seq, d = x.shape
    assert d == D_MODEL and seq % TM == 0 and mem.shape == (bsz, N_MEM, D_MODEL)
    assert w_in.shape[0] == 1
    tiles_per_seq = seq // TM
    n_tiles = bsz * tiles_per_seq
    row = lambda v: v.reshape(1, -1)
    bf = lambda w: w[0].astype(_BF16)

    k, v = pl.pallas_call(
        _kv_kernel,
        grid=(bsz,),
        in_specs=[pl.BlockSpec((None, N_MEM, D_MODEL), lambda b: (b, 0, 0)),
                  pl.BlockSpec((1, D_MODEL), lambda b: (0, 0)),
                  pl.BlockSpec((D_MODEL, 2 * D_MODEL), lambda b: (0, 0))],
        out_specs=[pl.BlockSpec((None, N_MEM, D_MODEL), lambda b: (b, 0, 0))] * 2,
        out_shape=[jax.ShapeDtypeStruct((bsz, N_MEM, D_MODEL), _BF16)] * 2,
        compiler_params=pltpu.CompilerParams(dimension_semantics=("arbitrary",),
                                             vmem_limit_bytes=VMEM_LIMIT_KV),
        name="kv_proj",
    )(mem, row(mem_norm_g[0]), bf(w_kv))

    def mix_rows(s):
        j = jnp.minimum(s, n_tiles - 1)
        return (j // tiles_per_seq, j % tiles_per_seq, 0)

    def out_rows(s):
        j = jnp.maximum(s - 1, 0)
        return (j // tiles_per_seq, j % tiles_per_seq, 0)

    kv_spec = pl.BlockSpec((None, N_MEM, D_MODEL), lambda s: (jnp.maximum(s - 1, 0) // tiles_per_seq, 0, 0))
    f32_scratch = lambda rows, cols: pltpu.VMEM((rows, cols), _F32)
    bf16_scratch = lambda rows, cols: pltpu.VMEM((rows, cols), _BF16)
    return pl.pallas_call(
        functools.partial(_layer_kernel, tiles_per_seq),
        grid=(n_tiles + 1,),
        in_specs=[pl.BlockSpec((None, TM, D_MODEL), mix_rows), kv_spec, kv_spec,
                  _const_spec((1, D_MODEL)),
                  _const_spec((D_MODEL, 2 * (D_LRU + D_CONF))),
                  _const_spec((LRU_CONV, D_LRU)),
                  _const_spec((1, D_LRU)),
                  _const_spec((D_LRU // HALF, HALF, 2 * HALF)),
                  _const_spec((1, D_LRU)),
                  _const_spec((1, D_LRU)),
                  _const_spec((1, D_LRU)),
                  _const_spec((CONF_CONV, D_CONF)),
                  _const_spec((1, D_CONF)),
                  _const_spec((1, D_CONF)),
                  _const_spec((1, D_CONF)),
                  _const_spec((D_LRU + D_CONF, D_MODEL)),
                  _const_spec((1, D_MODEL)),
                  _const_spec((D_MODEL, D_MODEL)),
                  _const_spec((D_MODEL, D_MODEL)),
                  _const_spec((1, D_MODEL)),
                  _const_spec((D_MODEL, 2 * D_FF)),
                  _const_spec((FFN_CONV, D_FF)),
                  _const_spec((1, D_FF)),
                  _const_spec((D_FF, D_MODEL)),
                  _const_spec((1, D_MODEL))],
        out_specs=pl.BlockSpec((None, TM, D_MODEL), out_rows),
        out_shape=jax.ShapeDtypeStruct(x.shape, x.dtype),
        scratch_shapes=[f32_scratch(TM, D_MODEL),
                        f32_scratch(TM, D_MODEL),
                        f32_scratch(SUBLANES + TM, D_LRU),
                        f32_scratch(CONF_HALO + TM, D_CONF),
                        f32_scratch(TM, D_CONF),
                        f32_scratch(TM, D_LRU),
                        f32_scratch(SUBLANES, D_LRU),
                        f32_scratch(MIX_ROWS, D_LRU),
                        f32_scratch(MIX_ROWS, D_LRU),
                        bf16_scratch(TM, D_LRU + D_CONF),
                        bf16_scratch(TM, D_MODEL),
                        f32_scratch(SUBLANES, D_FF),
                        pltpu.VMEM((2, SUBLANES + TM, FF_CHUNK), _F32),
                        pltpu.VMEM((2, TM, FF_CHUNK), _F32),
                        bf16_scratch(TM, D_FF)],
        compiler_params=pltpu.CompilerParams(dimension_semantics=("arbitrary",),
                                             vmem_limit_bytes=VMEM_LIMIT_LAYER),
        name="layer",
    )(x, k, v,
      row(mix_norm_g[0]), bf(w_in), lru_conv_w[0], row(lru_conv_b[0]),
      _block_diag_gates(lru_w_a[0], lru_w_x[0]).astype(_BF16), row(lru_b_a[0]), row(lru_b_x[0]),
      row(lru_lambda[0]), conf_conv_w[0], row(conf_conv_b[0]), row(conf_ln_g[0]), row(conf_ln_b[0]), bf(w_out),
      row(xa_norm_g[0]), bf(w_q), bf(w_o),
      row(ffn_norm_g[0]), bf(w_up), ffn_conv_w[0], row(ffn_conv_b[0]), bf(w_down), row(final_norm_g))
```

```python
import functools
import math

import jax
import jax.numpy as jnp
from jax import lax
from jax.experimental import pallas as pl
from jax.experimental.pallas import tpu as pltpu

D_MODEL = 1024
N_MEM = 256
D_LRU = 512
D_CONF = 512
LRU_HD = 64
LRU_CONV = 4
RG_C = 8.0
CONF_CONV = 31
XA_HEADS = 4
XA_HD = 256
D_FF = 3072
FFN_CONV = 3
EPS = 1e-6

SUBLANES = 8
HALF = 256
TM = 512
MIX_ROWS = TM
CONV_ROWS = TM
FF_CHUNK = 512
CONF_HALO = 32
VMEM_LIMIT_KV = 32 * 1024 * 1024
VMEM_LIMIT_LAYER = 62 * 1024 * 1024

_BF16 = jnp.bfloat16
_F32 = jnp.float32
_DONE = object()


def _rms(x, g):
    return x * lax.rsqrt(jnp.mean(x * x, axis=-1, keepdims=True) + EPS) * g


_LOG2E = math.log2(math.e)


def _gelu_tanh_times(x, y):
    c = math.sqrt(2.0 / math.pi)
    k1 = -2.0 * c * _LOG2E
    k3 = k1 * 0.044715
    e = jnp.exp2(x * (k1 + k3 * (x * x)))
    return (x * y) / (1.0 + e)


def _sigmoid(x):
    return 1.0 / (1.0 + jnp.exp2(x * (-_LOG2E)))


def _dot(a, b):
    return jnp.dot(a, b, preferred_element_type=_F32)


def _shift_rows_1(x):
    return pltpu.roll(x, 1, 0)


def _kv_kernel(mem_ref, g_ref, wkv_ref, k_ref, v_ref):
    m = _rms(mem_ref[...], g_ref[...]).astype(_BF16)
    kv = _dot(m, wkv_ref[...].astype(_BF16))
    k_ref[...] = (kv[:, :D_MODEL] * (XA_HD ** -0.5)).astype(_BF16)
    v_ref[...] = kv[:, D_MODEL:].astype(_BF16)


def _mix_carry_in(first_tile, xbuf, cbuf, hcar):
    @pl.when(first_tile)
    def _():
        xbuf[0:SUBLANES, :] = jnp.zeros((SUBLANES, D_LRU), _F32)
        cbuf[0:CONF_HALO, :] = jnp.zeros((CONF_HALO, D_CONF), _F32)
        hcar[...] = jnp.zeros((SUBLANES, D_LRU), _F32)

    @pl.when(jnp.logical_not(first_tile))
    def _():
        xbuf[0:SUBLANES, :] = xbuf[TM:TM + SUBLANES, :]
        cbuf[0:CONF_HALO, :] = cbuf[TM:TM + CONF_HALO, :]


def _conv4(r0, cw4_ref, cb4_ref, xbuf):
    ext = xbuf[r0:r0 + MIX_ROWS + SUBLANES, :]
    acc = cw4_ref[0:1, :] * ext
    for k in range(1, LRU_CONV):
        acc = _shift_rows_1(acc) + cw4_ref[k:k + 1, :] * ext
    return acc[SUBLANES:, :] + cb4_ref[...]


def _conv31(r0, n_rows, cw31_ref, cb31_ref, cbuf):
    ext_rows = n_rows + SUBLANES
    win = cbuf[r0:r0 + CONF_HALO + n_rows, :]
    sources = (win, pltpu.roll(win, 4, 0))
    parts = [[None, None], [None, None]]
    for m in range(2):
        for r in range(2):
            for q in range(CONF_HALO // SUBLANES):
                for s in range(2):
                    d = 8 * q + 4 * s + 2 * m + r
                    if d >= CONF_CONV:
                        continue
                    k = CONF_CONV - 1 - d
                    start = CONF_HALO - SUBLANES - SUBLANES * q
                    term = cw31_ref[k:k + 1, :] * sources[s][start:start + ext_rows, :]
                    parts[m][r] = term if parts[m][r] is None else parts[m][r] + term
    even = parts[0][0] + _shift_rows_1(parts[0][1])
    odd = parts[1][0] + _shift_rows_1(parts[1][1])
    acc = even + pltpu.roll(odd, 2, 0)
    return acc[SUBLANES:, :] + cb31_ref[...]


def _recurrence(a, u, carry, abuf, ubuf):
    groups = MIX_ROWS // SUBLANES
    a3 = a.reshape(groups, SUBLANES, D_LRU)
    u3 = u.reshape(groups, SUBLANES, D_LRU)
    sub = lax.broadcasted_iota(jnp.int32, (groups, SUBLANES, D_LRU), 1)
    for k in (1, 2, 4):
        keep = sub >= k
        a_prev = jnp.where(keep, pltpu.roll(a3, k, 1), 1.0)
        u_prev = jnp.where(keep, pltpu.roll(u3, k, 1), 0.0)
        u3 = u3 + a3 * u_prev
        a3 = a3 * a_prev
    abuf[...] = a3.reshape(MIX_ROWS, D_LRU)
    ubuf[...] = u3.reshape(MIX_ROWS, D_LRU)
    for gidx in range(groups):
        rows = pl.ds(gidx * SUBLANES, SUBLANES)
        hg = ubuf[rows, :] + abuf[rows, :] * carry
        ubuf[rows, :] = hg
        carry = jnp.broadcast_to(hg[SUBLANES - 1:SUBLANES, :], (SUBLANES, D_LRU))
    return ubuf[...], carry


def _mix_block(r0, state, x_ref, g_ref, win_ref, cw4_ref, cb4_ref, wgate_ref, ba_ref, bx_ref, lam_ref,
               cw31_ref, cb31_ref, lng_ref, lnb_ref, wout_ref, o_ref,
               xbuf, cbuf, cbbuf, gatebuf, hcar, abuf, ubuf, ycat):
    rows = slice(r0, r0 + MIX_ROWS)
    h = _rms(x_ref[rows, :], g_ref[...]).astype(_BF16)
    z = _dot(h, win_ref[:, 0:2 * D_LRU])
    xbuf[SUBLANES + r0:SUBLANES + r0 + MIX_ROWS, :] = z[:, 0:D_LRU]
    gatebuf[rows, :] = z[:, D_LRU:]
    yield

    xc = _conv4(r0, cw4_ref, cb4_ref, xbuf)
    xcb = xc.astype(_BF16)
    for j in range(D_LRU // HALF):
        gj = _dot(xcb[:, j * HALF:(j + 1) * HALF], wgate_ref[j])
        abuf[:, j * HALF:(j + 1) * HALF] = gj[:, :HALF]
        ubuf[:, j * HALF:(j + 1) * HALF] = gj[:, HALF:]
    r = _sigmoid(abuf[...] + ba_ref[...])
    i = _sigmoid(ubuf[...] + bx_ref[...])
    nlam = -lam_ref[...]
    softplus = jnp.maximum(nlam, 0.0) + jnp.log1p(jnp.exp(-jnp.abs(nlam)))
    log_a = (-RG_C * softplus) * r
    a = jnp.exp(log_a)
    u = jnp.sqrt(-jnp.tanh(log_a) * (a * a + 1.0)) * (i * xc)
    assert state["next_row"] == r0, "row blocks must run their second piece in order"
    carry = hcar[...] if r0 == 0 else state["carry"]
    hseq, carry = _recurrence(a, u, carry, abuf, ubuf)
    state.update(next_row=r0 + MIX_ROWS, carry=carry)
    if r0 + MIX_ROWS == TM:
        hcar[...] = carry
    ycat[rows, 0:D_LRU] = _gelu_tanh_times(gatebuf[rows, :], hseq).astype(_BF16)

    for c0 in range(r0, r0 + MIX_ROWS, CONV_ROWS):
        yield
        crows = slice(c0, c0 + CONV_ROWS)
        brows = slice(CONF_HALO + c0, CONF_HALO + c0 + CONV_ROWS)
        zc = _dot(h[c0 - r0:c0 - r0 + CONV_ROWS, :], win_ref[:, 2 * D_LRU:])
        cbuf[brows, :] = zc[:, 0:D_CONF]
        cbbuf[crows, :] = zc[:, D_CONF:]
        cbuf[brows, :] = cbuf[brows, :] * _sigmoid(cbbuf[crows, :])
        c = _conv31(c0, CONV_ROWS, cw31_ref, cb31_ref, cbuf)
        mu = jnp.mean(c, axis=-1, keepdims=True)
        cc = c - mu
        var = jnp.mean(cc * cc, axis=-1, keepdims=True)
        cn = cc * lax.rsqrt(var + EPS) * lng_ref[...] + lnb_ref[...]
        ycat[crows, D_LRU:] = (cn * _sigmoid(cn)).astype(_BF16)
    yield
    o_ref[rows, :] = _dot(ycat[rows, :], wout_ref[...])
    o_ref[rows, :] = o_ref[rows, :] + x_ref[rows, :]


def _xattn(x_ref, g_ref, wq_ref, k_ref, v_ref, wo_ref, o_ref, obuf, sbuf):
    h = _rms(x_ref[...], g_ref[...]).astype(_BF16)
    o_ref[...] = _dot(h, wq_ref[...])
    q = o_ref[...].astype(_BF16)
    yield
    heads = [slice(hd * XA_HD, (hd + 1) * XA_HD) for hd in range(XA_HEADS)]
    for hd, cols in enumerate(heads):
        sbuf[hd // 2, :, heads[hd % 2]] = lax.dot_general(
            q[:, cols], k_ref[:, cols], (((1,), (1,)), ((), ())), preferred_element_type=_F32)
    yield
    for hd, cols in enumerate(heads):
        s = sbuf[hd // 2, :, heads[hd % 2]]
        e = jnp.exp(s - jnp.max(s, axis=-1, keepdims=True))
        p = e * (1.0 / jnp.sum(e, axis=-1, keepdims=True))
        obuf[:, cols] = _dot(p.astype(_BF16), v_ref[:, cols]).astype(_BF16)
    yield
    o_ref[...] = _dot(obuf[...], wo_ref[...])
    o_ref[...] = o_ref[...] + x_ref[...]


def _ffn(x_ref, g_ref, wup_ref, cw_ref, cb_ref, wdown_ref, gf_ref, o_ref, ghalo, gbuf, ubuf2, actbuf):
    x = x_ref[...]
    h = _rms(x, g_ref[...]).astype(_BF16)
    for idx, c0 in enumerate(range(0, D_FF, FF_CHUNK)):
        cols = slice(c0, c0 + FF_CHUNK)
        slot = idx % 2
        gbuf[slot, 0:SUBLANES, :] = ghalo[:, cols]
        gbuf[slot, SUBLANES:SUBLANES + TM, :] = _dot(h, wup_ref[:, cols])
        ubuf2[slot] = _dot(h, wup_ref[:, D_FF + c0:D_FF + c0 + FF_CHUNK])
        ghalo[:, cols] = gbuf[slot, TM:TM + SUBLANES, :]
        ext = gbuf[slot]
        acc = cw_ref[0:1, cols] * ext
        for k in range(1, FFN_CONV):
            acc = _shift_rows_1(acc) + cw_ref[k:k + 1, cols] * ext
        gc = acc[SUBLANES:, :] + cb_ref[:, cols]
        actbuf[:, cols] = _gelu_tanh_times(gc, ubuf2[slot]).astype(_BF16)
        yield
    o_ref[...] = _dot(actbuf[...], wdown_ref[...])
    o_ref[...] = _rms(x + o_ref[...], gf_ref[...])


def _load_weights_as_bf16(weights, stages, sems):
    chunks = [(src, dst, r, c)
              for src, dst in weights
              for r in range(0, dst.shape[0], TM)
              for c in range(0, dst.shape[1], D_MODEL)]

    def chunk_copy(i):
        src, _, r, c = chunks[i]
        return pltpu.make_async_copy(src.at[0, pl.ds(r, TM), pl.ds(c, D_MODEL)], stages[i % 2], sems.at[i % 2])

    chunk_copy(0).start()
    for i, (_, dst, r, c) in enumerate(chunks):
        if i + 1 < len(chunks):
            chunk_copy(i + 1).start()
        chunk_copy(i).wait()
        dst[r:r + TM, c:c + D_MODEL] = stages[i % 2][...].astype(_BF16)


def _layer_kernel(tiles_per_seq,
                  x_ref, k_ref, v_ref,
                  g_mix, win_hbm, cw4_ref, cb4_ref, wgate_ref, ba_ref, bx_ref, lam_ref,
                  cw31_ref, cb31_ref, lng_ref, lnb_ref, wout_hbm,
                  g_xa, wq_hbm, wo_hbm,
                  g_ffn, wup_hbm, cwf_ref, cbf_ref, wdown_hbm, g_final,
                  o_ref,
                  win_ref, wout_ref, wq_ref, wo_ref, wup_ref, wdown_ref, wsem,
                  x1buf, x2buf, xbuf, cbuf, cbbuf, gatebuf, hcar, abuf, ubuf, ycat, obuf, ghalo, gbuf, ubuf2, actbuf):
    s = pl.program_id(0)
    n_tiles = pl.num_programs(0) - 1
    mix_tile = jnp.minimum(s, n_tiles - 1)
    out_tile = jnp.maximum(s - 1, 0)

    @pl.when(s == 0)
    def _():
        _load_weights_as_bf16([(win_hbm, win_ref), (wq_hbm, wq_ref), (wo_hbm, wo_ref), (wup_hbm, wup_ref),
                               (wdown_hbm, wdown_ref), (wout_hbm, wout_ref)], (x1buf, x2buf), wsem)
        x1buf[...] = jnp.zeros((TM, D_MODEL), _F32)

    @pl.when(out_tile % tiles_per_seq == 0)
    def _():
        ghalo[...] = jnp.zeros((SUBLANES, D_FF), _F32)

    _mix_carry_in(mix_tile % tiles_per_seq == 0, xbuf, cbuf, hcar)

    state = {"next_row": 0, "carry": None}
    (m0,) = [
        _mix_block(r0, state, x_ref, g_mix, win_ref, cw4_ref, cb4_ref, wgate_ref, ba_ref, bx_ref, lam_ref,
                   cw31_ref, cb31_ref, lng_ref, lnb_ref, wout_ref, x1buf,
                   xbuf, cbuf, cbbuf, gatebuf, hcar, abuf, ubuf, ycat)
        for r0 in range(0, TM, MIX_ROWS)]
    xa = _xattn(x1buf, g_xa, wq_ref, k_ref, v_ref, wo_ref, x2buf, obuf, ubuf2)
    ff = _ffn(x2buf, g_ffn, wup_ref, cwf_ref, cbf_ref, wdown_ref, g_final, o_ref, ghalo, gbuf, ubuf2, actbuf)
    schedule = [m0, xa,
                xa, m0, m0, xa,
                xa,
                ff, ff, ff, ff, ff, ff,
                ff, m0]
    stages = (m0, xa, ff)
    live = {id(g): True for g in stages}
    for g in schedule:
        live[id(g)] = next(g, _DONE) is not _DONE
    assert not any(live.values()), "schedule must exhaust every stage"


def _const_spec(shape):
    nd = len(shape)
    return pl.BlockSpec(shape, lambda s: (0,) * nd, pipeline_mode=pl.Buffered(1))


def _block_diag_gates(w_a, w_x):
    heads_per_half = HALF // LRU_HD
    eye = jnp.eye(heads_per_half, dtype=w_a.dtype)

    def bd(w):
        w = w.reshape(D_LRU // HALF, heads_per_half, LRU_HD, LRU_HD)
        return jnp.einsum("jhik,hg->jhigk", w, eye).reshape(D_LRU // HALF, HALF, HALF)

    return jnp.concatenate([bd(w_a), bd(w_x)], axis=-1)


def kernel(x, mem, mix_norm_g, w_in, lru_conv_w, lru_conv_b, lru_w_a, lru_b_a, lru_w_x, lru_b_x, lru_lambda, conf_conv_w, conf_conv_b, conf_ln_g, conf_ln_b, w_out, xa_norm_g, mem_norm_g, w_q, w_kv, w_o, ffn_norm_g, w_up, ffn_conv_w, ffn_conv_b, w_down, final_norm_g):
    bsz, seq, d = x.shape
    assert d == D_MODEL and seq % TM == 0 and mem.shape == (bsz, N_MEM, D_MODEL)
    assert w_in.shape[0] == 1
    tiles_per_seq = seq // TM
    n_tiles = bsz * tiles_per_seq
    row = lambda v: v.reshape(1, -1)

    k, v = pl.pallas_call(
        _kv_kernel,
        grid=(bsz,),
        in_specs=[pl.BlockSpec((None, N_MEM, D_MODEL), lambda b: (b, 0, 0)),
                  pl.BlockSpec((1, D_MODEL), lambda b: (0, 0)),
                  pl.BlockSpec((None, D_MODEL, 2 * D_MODEL), lambda b: (0, 0, 0), pipeline_mode=pl.Buffered(1))],
        out_specs=[pl.BlockSpec((None, N_MEM, D_MODEL), lambda b: (b, 0, 0))] * 2,
        out_shape=[jax.ShapeDtypeStruct((bsz, N_MEM, D_MODEL), _BF16)] * 2,
        compiler_params=pltpu.CompilerParams(dimension_semantics=("arbitrary",),
                                             vmem_limit_bytes=VMEM_LIMIT_KV),
        name="kv_proj",
    )(mem, row(mem_norm_g[0]), w_kv)

    def mix_rows(s):
        j = jnp.minimum(s, n_tiles - 1)
        return (j // tiles_per_seq, j % tiles_per_seq, 0)

    def out_rows(s):
        j = jnp.maximum(s - 1, 0)
        return (j // tiles_per_seq, j % tiles_per_seq, 0)

    kv_spec = pl.BlockSpec((None, N_MEM, D_MODEL), lambda s: (jnp.maximum(s - 1, 0) // tiles_per_seq, 0, 0))
    f32_scratch = lambda rows, cols: pltpu.VMEM((rows, cols), _F32)
    bf16_scratch = lambda rows, cols: pltpu.VMEM((rows, cols), _BF16)
    hbm = pl.BlockSpec(memory_space=pl.ANY)
    return pl.pallas_call(
        functools.partial(_layer_kernel, tiles_per_seq),
        grid=(n_tiles + 1,),
        in_specs=[pl.BlockSpec((None, TM, D_MODEL), mix_rows), kv_spec, kv_spec,
                  _const_spec((1, D_MODEL)),
                  hbm,
                  _const_spec((LRU_CONV, D_LRU)),
                  _const_spec((1, D_LRU)),
                  _const_spec((D_LRU // HALF, HALF, 2 * HALF)),
                  _const_spec((1, D_LRU)),
                  _const_spec((1, D_LRU)),
                  _const_spec((1, D_LRU)),
                  _const_spec((CONF_CONV, D_CONF)),
                  _const_spec((1, D_CONF)),
                  _const_spec((1, D_CONF)),
                  _const_spec((1, D_CONF)),
                  hbm,
                  _const_spec((1, D_MODEL)),
                  hbm, hbm,
                  _const_spec((1, D_MODEL)),
                  hbm,
                  _const_spec((FFN_CONV, D_FF)),
                  _const_spec((1, D_FF)),
                  hbm,
                  _const_spec((1, D_MODEL))],
        out_specs=pl.BlockSpec((None, TM, D_MODEL), out_rows),
        out_shape=jax.ShapeDtypeStruct(x.shape, x.dtype),
        scratch_shapes=[bf16_scratch(D_MODEL, 2 * (D_LRU + D_CONF)),
                        bf16_scratch(D_LRU + D_CONF, D_MODEL),
                        bf16_scratch(D_MODEL, D_MODEL),
                        bf16_scratch(D_MODEL, D_MODEL),
                        bf16_scratch(D_MODEL, 2 * D_FF),
                        bf16_scratch(D_FF, D_MODEL),
                        pltpu.SemaphoreType.DMA((2,)),
                        f32_scratch(TM, D_MODEL),
                        f32_scratch(TM, D_MODEL),
                        f32_scratch(SUBLANES + TM, D_LRU),
                        f32_scratch(CONF_HALO + TM, D_CONF),
                        f32_scratch(TM, D_CONF),
                        f32_scratch(TM, D_LRU),
                        f32_scratch(SUBLANES, D_LRU),
                        f32_scratch(MIX_ROWS, D_LRU),
                        f32_scratch(MIX_ROWS, D_LRU),
                        bf16_scratch(TM, D_LRU + D_CONF),
                        bf16_scratch(TM, D_MODEL),
                        f32_scratch(SUBLANES, D_FF),
                        pltpu.VMEM((2, SUBLANES + TM, FF_CHUNK), _F32),
                        pltpu.VMEM((2, TM, FF_CHUNK), _F32),
                        bf16_scratch(TM, D_FF)],
        compiler_params=pltpu.CompilerParams(dimension_semantics=("arbitrary",),
                                             vmem_limit_bytes=VMEM_LIMIT_LAYER),
        name="layer",
    )(x, k, v,
      row(mix_norm_g[0]), w_in, lru_conv_w[0], row(lru_conv_b[0]),
      _block_diag_gates(lru_w_a[0], lru_w_x[0]).astype(_BF16), row(lru_b_a[0]), row(lru_b_x[0]),
      row(lru_lambda[0]), conf_conv_w[0], row(conf_conv_b[0]), row(conf_ln_g[0]), row(conf_ln_b[0]), w_out,
      row(xa_norm_g[0]), w_q, w_o,
      row(ffn_norm_g[0]), w_up, ffn_conv_w[0], row(ffn_conv_b[0]), w_down, row(final_norm_g))
```

```python
import functools
import math

import jax
import jax.numpy as jnp
from jax import lax
from jax.experimental import pallas as pl
from jax.experimental.pallas import tpu as pltpu

D_MODEL = 1024
N_MEM = 256
D_LRU = 512
D_CONF = 512
LRU_HD = 64
LRU_CONV = 4
RG_C = 8.0
CONF_CONV = 31
XA_HEADS = 4
XA_HD = 256
D_FF = 3072
FFN_CONV = 3
EPS = 1e-6

SUBLANES = 8
HALF = 256
TM = 512
MIX_ROWS = TM
CONV_ROWS = TM
FF_CHUNK = 512
CONF_HALO = 32
VMEM_LIMIT_KV = 32 * 1024 * 1024
VMEM_LIMIT_LAYER = 63 * 1024 * 1024

_BF16 = jnp.bfloat16
_F32 = jnp.float32
_DONE = object()


def _rms(x, g):
    return x * lax.rsqrt(jnp.mean(x * x, axis=-1, keepdims=True) + EPS) * g


_LOG2E = math.log2(math.e)


def _gelu_tanh_times(x, y):
    c = math.sqrt(2.0 / math.pi)
    k1 = -2.0 * c * _LOG2E
    k3 = k1 * 0.044715
    e = jnp.exp2(x * (k1 + k3 * (x * x)))
    return (x * y) / (1.0 + e)


def _sigmoid(x):
    return 1.0 / (1.0 + jnp.exp2(x * (-_LOG2E)))


def _dot(a, b):
    return jnp.dot(a, b, preferred_element_type=_F32)


def _shift_rows_1(x):
    return pltpu.roll(x, 1, 0)


def _kv_kernel(mem_ref, g_ref, wkv_ref, k_ref, v_ref):
    m = _rms(mem_ref[...], g_ref[...]).astype(_BF16)
    kv = _dot(m, wkv_ref[...].astype(_BF16))
    k_ref[...] = (kv[:, :D_MODEL] * (XA_HD ** -0.5)).astype(_BF16)
    v_ref[...] = kv[:, D_MODEL:].astype(_BF16)


def _mix_carry_in(first_tile, xbuf, cbuf, hcar):
    @pl.when(first_tile)
    def _():
        xbuf[0:SUBLANES, :] = jnp.zeros((SUBLANES, D_LRU), _F32)
        cbuf[0:CONF_HALO, :] = jnp.zeros((CONF_HALO, D_CONF), _F32)
        hcar[...] = jnp.zeros((SUBLANES, D_LRU), _F32)

    @pl.when(jnp.logical_not(first_tile))
    def _():
        xbuf[0:SUBLANES, :] = xbuf[TM:TM + SUBLANES, :]
        cbuf[0:CONF_HALO, :] = cbuf[TM:TM + CONF_HALO, :]


def _conv4(r0, cw4_ref, cb4_ref, xbuf):
    ext = xbuf[r0:r0 + MIX_ROWS + SUBLANES, :]
    acc = cw4_ref[0:1, :] * ext
    for k in range(1, LRU_CONV):
        acc = _shift_rows_1(acc) + cw4_ref[k:k + 1, :] * ext
    return acc[SUBLANES:, :] + cb4_ref[...]


def _conv31(r0, n_rows, cw31_ref, cb31_ref, cbuf):
    ext_rows = n_rows + SUBLANES
    win = cbuf[r0:r0 + CONF_HALO + n_rows, :]
    sources = (win, pltpu.roll(win, 4, 0))
    parts = [[None, None], [None, None]]
    for m in range(2):
        for r in range(2):
            for q in range(CONF_HALO // SUBLANES):
                for s in range(2):
                    d = 8 * q + 4 * s + 2 * m + r
                    if d >= CONF_CONV:
                        continue
                    k = CONF_CONV - 1 - d
                    start = CONF_HALO - SUBLANES - SUBLANES * q
                    term = cw31_ref[k:k + 1, :] * sources[s][start:start + ext_rows, :]
                    parts[m][r] = term if parts[m][r] is None else parts[m][r] + term
    even = parts[0][0] + _shift_rows_1(parts[0][1])
    odd = parts[1][0] + _shift_rows_1(parts[1][1])
    acc = even + pltpu.roll(odd, 2, 0)
    return acc[SUBLANES:, :] + cb31_ref[...]


def _recurrence(a, u, carry, abuf, ubuf):
    groups = MIX_ROWS // SUBLANES
    a3 = a.reshape(groups, SUBLANES, D_LRU)
    u3 = u.reshape(groups, SUBLANES, D_LRU)
    sub = lax.broadcasted_iota(jnp.int32, (groups, SUBLANES, D_LRU), 1)
    for k in (1, 2, 4):
        keep = sub >= k
        a_prev = jnp.where(keep, pltpu.roll(a3, k, 1), 1.0)
        u_prev = jnp.where(keep, pltpu.roll(u3, k, 1), 0.0)
        u3 = u3 + a3 * u_prev
        a3 = a3 * a_prev
    abuf[...] = a3.reshape(MIX_ROWS, D_LRU)
    ubuf[...] = u3.reshape(MIX_ROWS, D_LRU)
    for gidx in range(groups):
        rows = pl.ds(gidx * SUBLANES, SUBLANES)
        hg = ubuf[rows, :] + abuf[rows, :] * carry
        ubuf[rows, :] = hg
        carry = jnp.broadcast_to(hg[SUBLANES - 1:SUBLANES, :], (SUBLANES, D_LRU))
    return ubuf[...], carry


def _mix_block(r0, state, x_ref, g_ref, win_ref, cw4_ref, cb4_ref, wgate_ref, ba_ref, bx_ref, lam_ref,
               cw31_ref, cb31_ref, lng_ref, lnb_ref, wout_ref, o_ref,
               xbuf, cbuf, cbbuf, gatebuf, hcar, abuf, ubuf, ycat):
    rows = slice(r0, r0 + MIX_ROWS)
    h = _rms(x_ref[rows, :], g_ref[...]).astype(_BF16)
    z = _dot(h, win_ref[:, 0:2 * D_LRU])
    xbuf[SUBLANES + r0:SUBLANES + r0 + MIX_ROWS, :] = z[:, 0:D_LRU]
    gatebuf[rows, :] = z[:, D_LRU:]
    yield

    xc = _conv4(r0, cw4_ref, cb4_ref, xbuf)
    xcb = xc.astype(_BF16)
    for j in range(D_LRU // HALF):
        gj = _dot(xcb[:, j * HALF:(j + 1) * HALF], wgate_ref[j])
        abuf[:, j * HALF:(j + 1) * HALF] = gj[:, :HALF]
        ubuf[:, j * HALF:(j + 1) * HALF] = gj[:, HALF:]
    r = _sigmoid(abuf[...] + ba_ref[...])
    i = _sigmoid(ubuf[...] + bx_ref[...])
    nlam = -lam_ref[...]
    softplus = jnp.maximum(nlam, 0.0) + jnp.log1p(jnp.exp(-jnp.abs(nlam)))
    log_a = (-RG_C * softplus) * r
    a = jnp.exp(log_a)
    u = jnp.sqrt(-jnp.tanh(log_a) * (a * a + 1.0)) * (i * xc)
    assert state["next_row"] == r0, "row blocks must run their second piece in order"
    carry = hcar[...] if r0 == 0 else state["carry"]
    hseq, carry = _recurrence(a, u, carry, abuf, ubuf)
    state.update(next_row=r0 + MIX_ROWS, carry=carry)
    if r0 + MIX_ROWS == TM:
        hcar[...] = carry
    ycat[rows, 0:D_LRU] = _gelu_tanh_times(gatebuf[rows, :], hseq).astype(_BF16)

    for c0 in range(r0, r0 + MIX_ROWS, CONV_ROWS):
        yield
        crows = slice(c0, c0 + CONV_ROWS)
        brows = slice(CONF_HALO + c0, CONF_HALO + c0 + CONV_ROWS)
        zc = _dot(h[c0 - r0:c0 - r0 + CONV_ROWS, :], win_ref[:, 2 * D_LRU:])
        cbuf[brows, :] = zc[:, 0:D_CONF]
        cbbuf[crows, :] = zc[:, D_CONF:]
        cbuf[brows, :] = cbuf[brows, :] * _sigmoid(cbbuf[crows, :])
        c = _conv31(c0, CONV_ROWS, cw31_ref, cb31_ref, cbuf)
        mu = jnp.mean(c, axis=-1, keepdims=True)
        cc = c - mu
        var = jnp.mean(cc * cc, axis=-1, keepdims=True)
        cn = cc * lax.rsqrt(var + EPS) * lng_ref[...] + lnb_ref[...]
        ycat[crows, D_LRU:] = (cn * _sigmoid(cn)).astype(_BF16)
    yield
    o_ref[rows, :] = _dot(ycat[rows, :], wout_ref[...])
    o_ref[rows, :] = o_ref[rows, :] + x_ref[rows, :]


def _xattn(x_ref, g_ref, wq_ref, k_ref, v_ref, wo_ref, o_ref, obuf, sbuf):
    h = _rms(x_ref[...], g_ref[...]).astype(_BF16)
    o_ref[...] = _dot(h, wq_ref[...])
    q = o_ref[...].astype(_BF16)
    yield
    heads = [slice(hd * XA_HD, (hd + 1) * XA_HD) for hd in range(XA_HEADS)]
    for hd, cols in enumerate(heads):
        sbuf[hd // 2, 0:TM, heads[hd % 2]] = lax.dot_general(
            q[:, cols], k_ref[:, cols], (((1,), (1,)), ((), ())), preferred_element_type=_F32)
    yield
    for hd, cols in enumerate(heads):
        s = sbuf[hd // 2, 0:TM, heads[hd % 2]]
        e = jnp.exp(s - jnp.max(s, axis=-1, keepdims=True))
        p = e * (1.0 / jnp.sum(e, axis=-1, keepdims=True))
        obuf[:, cols] = _dot(p.astype(_BF16), v_ref[:, cols]).astype(_BF16)
    yield
    o_ref[...] = _dot(obuf[...], wo_ref[...])
    o_ref[...] = o_ref[...] + x_ref[...]


def _ffn(x_ref, g_ref, wup_ref, cw_ref, cb_ref, wdown_ref, gf_ref, o_ref, ghalo, gbuf, ubuf2, actbuf):
    h = _rms(x_ref[...], g_ref[...]).astype(_BF16)
    for idx, c0 in enumerate(range(0, D_FF, FF_CHUNK)):
        cols = slice(c0, c0 + FF_CHUNK)
        slot = idx % 2
        gbuf[slot, 0:SUBLANES, :] = ghalo[:, cols]
        gbuf[slot, SUBLANES:SUBLANES + TM, :] = _dot(h, wup_ref[:, cols])
        ubuf2[...] = _dot(h, wup_ref[:, D_FF + c0:D_FF + c0 + FF_CHUNK])
        ghalo[:, cols] = gbuf[slot, TM:TM + SUBLANES, :]
        ext = gbuf[slot]
        acc = cw_ref[0:1, cols] * ext
        for k in range(1, FFN_CONV):
            acc = _shift_rows_1(acc) + cw_ref[k:k + 1, cols] * ext
        gc = acc[SUBLANES:, :] + cb_ref[:, cols]
        actbuf[:, cols] = _gelu_tanh_times(gc, ubuf2[...]).astype(_BF16)
        yield
    o_ref[...] = _dot(actbuf[...], wdown_ref[...])
    o_ref[...] = _rms(x_ref[...] + o_ref[...], gf_ref[...])


def _load_weights_as_bf16(weights, stages, sems):
    chunks = [(src, dst, r, c)
              for src, dst in weights
              for r in range(0, dst.shape[0], TM)
              for c in range(0, dst.shape[1], D_MODEL)]

    def chunk_copy(i):
        src, _, r, c = chunks[i]
        return pltpu.make_async_copy(src.at[0, pl.ds(r, TM), pl.ds(c, D_MODEL)], stages[i % 2], sems.at[i % 2])

    chunk_copy(0).start()
    for i, (_, dst, r, c) in enumerate(chunks):
        if i + 1 < len(chunks):
            chunk_copy(i + 1).start()
        chunk_copy(i).wait()
        dst[r:r + TM, c:c + D_MODEL] = stages[i % 2][...].astype(_BF16)


def _layer_kernel(tiles_per_seq,
                  x_ref, k_ref, v_ref,
                  g_mix, win_hbm, cw4_ref, cb4_ref, wgate_ref, ba_ref, bx_ref, lam_ref,
                  cw31_ref, cb31_ref, lng_ref, lnb_ref, wout_hbm,
                  g_xa, wq_hbm, wo_hbm,
                  g_ffn, wup_hbm, cwf_ref, cbf_ref, wdown_hbm, g_final,
                  o_ref,
                  win_ref, wout_ref, wq_ref, wo_ref, wup_ref, wdown_ref, wsem,
                  x1buf, x2buf, xbuf, cbuf, cbbuf, gatebuf, hcar, abuf, ubuf, ycat, obuf, ghalo, gbuf, ubuf2, actbuf):
    s = pl.program_id(0)
    n_tiles = pl.num_programs(0) - 1
    mix_tile = jnp.minimum(s, n_tiles - 1)
    out_tile = jnp.maximum(s - 1, 0)

    @pl.when(s == 0)
    def _():
        _load_weights_as_bf16([(win_hbm, win_ref), (wq_hbm, wq_ref), (wo_hbm, wo_ref), (wup_hbm, wup_ref),
                               (wdown_hbm, wdown_ref), (wout_hbm, wout_ref)], (x1buf, x2buf), wsem)

    @pl.when(out_tile % tiles_per_seq == 0)
    def _():
        ghalo[...] = jnp.zeros((SUBLANES, D_FF), _F32)

    _mix_carry_in(mix_tile % tiles_per_seq == 0, xbuf, cbuf, hcar)

    def step(with_mix, with_rest):
        state = {"next_row": 0, "carry": None}
        m0 = _mix_block(0, state, x_ref, g_mix, win_ref, cw4_ref, cb4_ref, wgate_ref, ba_ref, bx_ref, lam_ref,
                        cw31_ref, cb31_ref, lng_ref, lnb_ref, wout_ref, x1buf,
                        xbuf, cbuf, cbbuf, gatebuf, hcar, abuf, ubuf, ycat)
        xa = _xattn(x1buf, g_xa, wq_ref, k_ref, v_ref, wo_ref, x2buf, obuf, gbuf)
        ff = _ffn(x2buf, g_ffn, wup_ref, cwf_ref, cbf_ref, wdown_ref, g_final, o_ref, ghalo, gbuf, ubuf2, actbuf)
        schedule = [m0, xa,
                    xa, m0, m0, xa,
                    xa,
                    ff, ff, ff, ff, ff, ff,
                    ff, m0]
        stages = ((m0,) if with_mix else ()) + ((xa, ff) if with_rest else ())
        live = {id(g): True for g in stages}
        for g in schedule:
            if id(g) in live:
                live[id(g)] = next(g, _DONE) is not _DONE
        assert not any(live.values()), "schedule must exhaust every stage"

    pl.when(s == 0)(functools.partial(step, True, False))
    pl.when(jnp.logical_and(s > 0, s < n_tiles))(functools.partial(step, True, True))
    pl.when(s == n_tiles)(functools.partial(step, False, True))


def _const_spec(shape):
    nd = len(shape)
    return pl.BlockSpec(shape, lambda s: (0,) * nd, pipeline_mode=pl.Buffered(1))


def _block_diag_gates(w_a, w_x):
    heads_per_half = HALF // LRU_HD
    eye = jnp.eye(heads_per_half, dtype=w_a.dtype)

    def bd(w):
        w = w.reshape(D_LRU // HALF, heads_per_half, LRU_HD, LRU_HD)
        return jnp.einsum("jhik,hg->jhigk", w, eye).reshape(D_LRU // HALF, HALF, HALF)

    return jnp.concatenate([bd(w_a), bd(w_x)], axis=-1)


def kernel(x, mem, mix_norm_g, w_in, lru_conv_w, lru_conv_b, lru_w_a, lru_b_a, lru_w_x, lru_b_x, lru_lambda, conf_conv_w, conf_conv_b, conf_ln_g, conf_ln_b, w_out, xa_norm_g, mem_norm_g, w_q, w_kv, w_o, ffn_norm_g, w_up, ffn_conv_w, ffn_conv_b, w_down, final_norm_g):
    bsz, seq, d = x.shape
    assert d == D_MODEL and seq % TM == 0 and mem.shape == (bsz, N_MEM, D_MODEL)
    assert w_in.shape[0] == 1
    tiles_per_seq = seq // TM
    n_tiles = bsz * tiles_per_seq
    row = lambda v: v.reshape(1, -1)

    k, v = pl.pallas_call(
        _kv_kernel,
        grid=(bsz,),
        in_specs=[pl.BlockSpec((None, N_MEM, D_MODEL), lambda b: (b, 0, 0)),
                  pl.BlockSpec((1, D_MODEL), lambda b: (0, 0)),
                  pl.BlockSpec((None, D_MODEL, 2 * D_MODEL), lambda b: (0, 0, 0), pipeline_mode=pl.Buffered(1))],
        out_specs=[pl.BlockSpec((None, N_MEM, D_MODEL), lambda b: (b, 0, 0))] * 2,
        out_shape=[jax.ShapeDtypeStruct((bsz, N_MEM, D_MODEL), _BF16)] * 2,
        compiler_params=pltpu.CompilerParams(dimension_semantics=("arbitrary",),
                                             vmem_limit_bytes=VMEM_LIMIT_KV),
        name="kv_proj",
    )(mem, row(mem_norm_g[0]), w_kv)

    def mix_rows(s):
        j = jnp.minimum(s, n_tiles - 1)
        return (j // tiles_per_seq, j % tiles_per_seq, 0)

    def out_rows(s):
        j = jnp.maximum(s - 1, 0)
        return (j // tiles_per_seq, j % tiles_per_seq, 0)

    kv_spec = pl.BlockSpec((None, N_MEM, D_MODEL), lambda s: (jnp.maximum(s - 1, 0) // tiles_per_seq, 0, 0),
                           pipeline_mode=pl.Buffered(1))
    f32_scratch = lambda rows, cols: pltpu.VMEM((rows, cols), _F32)
    bf16_scratch = lambda rows, cols: pltpu.VMEM((rows, cols), _BF16)
    hbm = pl.BlockSpec(memory_space=pl.ANY)
    return pl.pallas_call(
        functools.partial(_layer_kernel, tiles_per_seq),
        grid=(n_tiles + 1,),
        in_specs=[pl.BlockSpec((None, TM, D_MODEL), mix_rows), kv_spec, kv_spec,
                  _const_spec((1, D_MODEL)),
                  hbm,
                  _const_spec((LRU_CONV, D_LRU)),
                  _const_spec((1, D_LRU)),
                  _const_spec((D_LRU // HALF, HALF, 2 * HALF)),
                  _const_spec((1, D_LRU)),
                  _const_spec((1, D_LRU)),
                  _const_spec((1, D_LRU)),
                  _const_spec((CONF_CONV, D_CONF)),
                  _const_spec((1, D_CONF)),
                  _const_spec((1, D_CONF)),
                  _const_spec((1, D_CONF)),
                  hbm,
                  _const_spec((1, D_MODEL)),
                  hbm, hbm,
                  _const_spec((1, D_MODEL)),
                  hbm,
                  _const_spec((FFN_CONV, D_FF)),
                  _const_spec((1, D_FF)),
                  hbm,
                  _const_spec((1, D_MODEL))],
        out_specs=pl.BlockSpec((None, TM, D_MODEL), out_rows),
        out_shape=jax.ShapeDtypeStruct(x.shape, x.dtype),
        scratch_shapes=[bf16_scratch(D_MODEL, 2 * (D_LRU + D_CONF)),
                        bf16_scratch(D_LRU + D_CONF, D_MODEL),
                        bf16_scratch(D_MODEL, D_MODEL),
                        bf16_scratch(D_MODEL, D_MODEL),
                        bf16_scratch(D_MODEL, 2 * D_FF),
                        bf16_scratch(D_FF, D_MODEL),
                        pltpu.SemaphoreType.DMA((2,)),
                        f32_scratch(TM, D_MODEL),
                        f32_scratch(TM, D_MODEL),
                        f32_scratch(SUBLANES + TM, D_LRU),
                        f32_scratch(CONF_HALO + TM, D_CONF),
                        f32_scratch(TM, D_CONF),
                        f32_scratch(TM, D_LRU),
                        f32_scratch(SUBLANES, D_LRU),
                        f32_scratch(MIX_ROWS, D_LRU),
                        f32_scratch(MIX_ROWS, D_LRU),
                        bf16_scratch(TM, D_LRU + D_CONF),
                        bf16_scratch(TM, D_MODEL),
                        f32_scratch(SUBLANES, D_FF),
                        pltpu.VMEM((2, SUBLANES + TM, FF_CHUNK), _F32),
                        f32_scratch(TM, FF_CHUNK),
                        bf16_scratch(TM, D_FF)],
        compiler_params=pltpu.CompilerParams(dimension_semantics=("arbitrary",),
                                             vmem_limit_bytes=VMEM_LIMIT_LAYER),
        name="layer",
    )(x, k, v,
      row(mix_norm_g[0]), w_in, lru_conv_w[0], row(lru_conv_b[0]),
      _block_diag_gates(lru_w_a[0], lru_w_x[0]).astype(_BF16), row(lru_b_a[0]), row(lru_b_x[0]),
      row(lru_lambda[0]), conf_conv_w[0], row(conf_conv_b[0]), row(conf_ln_g[0]), row(conf_ln_b[0]), w_out,
      row(xa_norm_g[0]), w_q, w_o,
      row(ffn_norm_g[0]), w_up, ffn_conv_w[0], row(ffn_conv_b[0]), w_down, row(final_norm_g))
```

```python
import functools
import math

import jax
import jax.numpy as jnp
from jax import lax
from jax.experimental import pallas as pl
from jax.experimental.pallas import tpu as pltpu

D_MODEL = 1024
N_MEM = 256
D_LRU = 512
D_CONF = 512
LRU_HD = 64
LRU_CONV = 4
RG_C = 8.0
CONF_CONV = 31
XA_HEADS = 4
XA_HD = 256
D_FF = 3072
FFN_CONV = 3
EPS = 1e-6

SUBLANES = 8
HALF = 256
TM = 512
MIX_ROWS = TM
CONV_ROWS = TM
FF_CHUNK = 512
CONF_HALO = 32
VMEM_LIMIT_KV = 32 * 1024 * 1024
VMEM_LIMIT_LAYER = 63 * 1024 * 1024

_BF16 = jnp.bfloat16
_F32 = jnp.float32
_DONE = object()


def _rms(x, g):
    return x * lax.rsqrt(jnp.mean(x * x, axis=-1, keepdims=True) + EPS) * g


_LOG2E = math.log2(math.e)


def _gelu_tanh_times(x, y):
    c = math.sqrt(2.0 / math.pi)
    k1 = -2.0 * c * _LOG2E
    k3 = k1 * 0.044715
    e = jnp.exp2(x * (k1 + k3 * (x * x)))
    return (x * y) / (1.0 + e)


def _sigmoid(x):
    return 1.0 / (1.0 + jnp.exp2(x * (-_LOG2E)))


def _dot(a, b):
    return jnp.dot(a, b, preferred_element_type=_F32)


def _shift_rows_1(x):
    return pltpu.roll(x, 1, 0)


def _kv_kernel(mem_ref, g_ref, wkv_ref, k_ref, v_ref):
    m = _rms(mem_ref[...], g_ref[...]).astype(_BF16)
    kv = _dot(m, wkv_ref[...].astype(_BF16))
    k_ref[...] = (kv[:, :D_MODEL] * (XA_HD ** -0.5)).astype(_BF16)
    v_ref[...] = kv[:, D_MODEL:].astype(_BF16)


def _mix_carry_in(first_tile, xbuf, cbuf, hcar):
    @pl.when(first_tile)
    def _():
        xbuf[0:SUBLANES, :] = jnp.zeros((SUBLANES, D_LRU), _F32)
        cbuf[0:CONF_HALO, :] = jnp.zeros((CONF_HALO, D_CONF), _F32)
        hcar[...] = jnp.zeros((SUBLANES, D_LRU), _F32)

    @pl.when(jnp.logical_not(first_tile))
    def _():
        xbuf[0:SUBLANES, :] = xbuf[TM:TM + SUBLANES, :]
        cbuf[0:CONF_HALO, :] = cbuf[TM:TM + CONF_HALO, :]


def _conv4(r0, cw4_ref, cb4_ref, xbuf):
    ext = xbuf[r0:r0 + MIX_ROWS + SUBLANES, :]
    acc = cw4_ref[0:1, :] * ext
    for k in range(1, LRU_CONV):
        acc = _shift_rows_1(acc) + cw4_ref[k:k + 1, :] * ext
    return acc[SUBLANES:, :] + cb4_ref[...]


def _conv31(r0, n_rows, cw31_ref, cb31_ref, cbuf):
    ext_rows = n_rows + SUBLANES
    win = cbuf[r0:r0 + CONF_HALO + n_rows, :]
    sources = (win, pltpu.roll(win, 4, 0))
    parts = [[None, None], [None, None]]
    for m in range(2):
        for r in range(2):
            for q in range(CONF_HALO // SUBLANES):
                for s in range(2):
                    d = 8 * q + 4 * s + 2 * m + r
                    if d >= CONF_CONV:
                        continue
                    k = CONF_CONV - 1 - d
                    start = CONF_HALO - SUBLANES - SUBLANES * q
                    term = cw31_ref[k:k + 1, :] * sources[s][start:start + ext_rows, :]
                    parts[m][r] = term if parts[m][r] is None else parts[m][r] + term
    even = parts[0][0] + _shift_rows_1(parts[0][1])
    odd = parts[1][0] + _shift_rows_1(parts[1][1])
    acc = even + pltpu.roll(odd, 2, 0)
    return acc[SUBLANES:, :] + cb31_ref[...]


def _recurrence(a, u, carry, abuf, ubuf):
    groups = MIX_ROWS // SUBLANES
    a3 = a.reshape(groups, SUBLANES, D_LRU)
    u3 = u.reshape(groups, SUBLANES, D_LRU)
    sub = lax.broadcasted_iota(jnp.int32, (groups, SUBLANES, D_LRU), 1)
    for k in (1, 2, 4):
        keep = sub >= k
        a_prev = jnp.where(keep, pltpu.roll(a3, k, 1), 1.0)
        u_prev = jnp.where(keep, pltpu.roll(u3, k, 1), 0.0)
        u3 = u3 + a3 * u_prev
        a3 = a3 * a_prev
    abuf[...] = a3.reshape(MIX_ROWS, D_LRU)
    ubuf[...] = u3.reshape(MIX_ROWS, D_LRU)
    for gidx in range(groups):
        rows = pl.ds(gidx * SUBLANES, SUBLANES)
        hg = ubuf[rows, :] + abuf[rows, :] * carry
        ubuf[rows, :] = hg
        carry = jnp.broadcast_to(hg[SUBLANES - 1:SUBLANES, :], (SUBLANES, D_LRU))
    return ubuf[...], carry


def _mix_block(r0, state, x_ref, g_ref, win_ref, cw4_ref, cb4_ref, wgate_ref, ba_ref, bx_ref, lam_ref,
               cw31_ref, cb31_ref, lng_ref, lnb_ref, wout_ref, o_ref,
               xbuf, cbuf, cbbuf, gatebuf, hcar, abuf, ubuf, ycat):
    rows = slice(r0, r0 + MIX_ROWS)
    h = _rms(x_ref[rows, :], g_ref[...]).astype(_BF16)
    z = _dot(h, win_ref[:, 0:2 * D_LRU])
    xbuf[SUBLANES + r0:SUBLANES + r0 + MIX_ROWS, :] = z[:, 0:D_LRU]
    gatebuf[rows, :] = z[:, D_LRU:]
    yield

    xc = _conv4(r0, cw4_ref, cb4_ref, xbuf)
    xcb = xc.astype(_BF16)
    for j in range(D_LRU // HALF):
        gj = _dot(xcb[:, j * HALF:(j + 1) * HALF], wgate_ref[j])
        abuf[:, j * HALF:(j + 1) * HALF] = gj[:, :HALF]
        ubuf[:, j * HALF:(j + 1) * HALF] = gj[:, HALF:]
    r = _sigmoid(abuf[...] + ba_ref[...])
    i = _sigmoid(ubuf[...] + bx_ref[...])
    nlam = -lam_ref[...]
    softplus = jnp.maximum(nlam, 0.0) + jnp.log1p(jnp.exp(-jnp.abs(nlam)))
    log_a = (-RG_C * softplus) * r
    a = jnp.exp(log_a)
    u = jnp.sqrt(-jnp.tanh(log_a) * (a * a + 1.0)) * (i * xc)
    assert state["next_row"] == r0, "row blocks must run their second piece in order"
    carry = hcar[...] if r0 == 0 else state["carry"]
    hseq, carry = _recurrence(a, u, carry, abuf, ubuf)
    state.update(next_row=r0 + MIX_ROWS, carry=carry)
    if r0 + MIX_ROWS == TM:
        hcar[...] = carry
    ycat[rows, 0:D_LRU] = _gelu_tanh_times(gatebuf[rows, :], hseq).astype(_BF16)

    for c0 in range(r0, r0 + MIX_ROWS, CONV_ROWS):
        yield
        crows = slice(c0, c0 + CONV_ROWS)
        brows = slice(CONF_HALO + c0, CONF_HALO + c0 + CONV_ROWS)
        zc = _dot(h[c0 - r0:c0 - r0 + CONV_ROWS, :], win_ref[:, 2 * D_LRU:])
        cbuf[brows, :] = zc[:, 0:D_CONF]
        cbbuf[crows, :] = zc[:, D_CONF:]
        cbuf[brows, :] = cbuf[brows, :] * _sigmoid(cbbuf[crows, :])
        c = _conv31(c0, CONV_ROWS, cw31_ref, cb31_ref, cbuf)
        mu = jnp.mean(c, axis=-1, keepdims=True)
        cc = c - mu
        var = jnp.mean(cc * cc, axis=-1, keepdims=True)
        cn = cc * lax.rsqrt(var + EPS) * lng_ref[...] + lnb_ref[...]
        ycat[crows, D_LRU:] = (cn * _sigmoid(cn)).astype(_BF16)
    yield
    o_ref[rows, :] = _dot(ycat[rows, :], wout_ref[...])
    o_ref[rows, :] = o_ref[rows, :] + x_ref[rows, :]


def _xattn(x_ref, g_ref, wq_ref, k_ref, v_ref, wo_ref, o_ref, obuf, sbuf):
    h = _rms(x_ref[...], g_ref[...]).astype(_BF16)
    o_ref[...] = _dot(h, wq_ref[...])
    q = o_ref[...].astype(_BF16)
    yield
    heads = [slice(hd * XA_HD, (hd + 1) * XA_HD) for hd in range(XA_HEADS)]
    for hd, cols in enumerate(heads):
        sbuf[hd // 2, 0:TM, heads[hd % 2]] = lax.dot_general(
            q[:, cols], k_ref[:, cols], (((1,), (1,)), ((), ())), preferred_element_type=_F32)
    yield
    for hd, cols in enumerate(heads):
        s = sbuf[hd // 2, 0:TM, heads[hd % 2]]
        e = jnp.exp(s - jnp.max(s, axis=-1, keepdims=True))
        p = e * (1.0 / jnp.sum(e, axis=-1, keepdims=True))
        obuf[:, cols] = _dot(p.astype(_BF16), v_ref[:, cols]).astype(_BF16)
    yield
    o_ref[...] = _dot(obuf[...], wo_ref[...])
    o_ref[...] = o_ref[...] + x_ref[...]


def _ffn(x_ref, g_ref, wup_ref, cw_ref, cb_ref, wdown_ref, gf_ref, o_ref, ghalo, gbuf, ubuf2, actbuf):
    h = _rms(x_ref[...], g_ref[...]).astype(_BF16)
    for idx, c0 in enumerate(range(0, D_FF, FF_CHUNK)):
        cols = slice(c0, c0 + FF_CHUNK)
        slot = idx % 2
        gbuf[slot, 0:SUBLANES, :] = ghalo[:, cols]
        gbuf[slot, SUBLANES:SUBLANES + TM, :] = _dot(h, wup_ref[:, cols])
        ubuf2[...] = _dot(h, wup_ref[:, D_FF + c0:D_FF + c0 + FF_CHUNK])
        ghalo[:, cols] = gbuf[slot, TM:TM + SUBLANES, :]
        ext = gbuf[slot]
        acc = cw_ref[0:1, cols] * ext
        for k in range(1, FFN_CONV):
            acc = _shift_rows_1(acc) + cw_ref[k:k + 1, cols] * ext
        gc = acc[SUBLANES:, :] + cb_ref[:, cols]
        actbuf[:, cols] = _gelu_tanh_times(gc, ubuf2[...]).astype(_BF16)
        yield
    o_ref[...] = _dot(actbuf[...], wdown_ref[...])
    o_ref[...] = _rms(x_ref[...] + o_ref[...], gf_ref[...])


def _load_weights_as_bf16(weights, stages, sems):
    chunks = [(src, dst, r, c)
              for src, dst in weights
              for r in range(0, dst.shape[0], TM)
              for c in range(0, dst.shape[1], D_MODEL)]

    def chunk_copy(i):
        src, _, r, c = chunks[i]
        return pltpu.make_async_copy(src.at[0, pl.ds(r, TM), pl.ds(c, D_MODEL)], stages[i % 2], sems.at[i % 2])

    chunk_copy(0).start()
    for i, (_, dst, r, c) in enumerate(chunks):
        if i + 1 < len(chunks):
            chunk_copy(i + 1).start()
        chunk_copy(i).wait()
        dst[r:r + TM, c:c + D_MODEL] = stages[i % 2][...].astype(_BF16)


def _layer_kernel(tiles_per_seq,
                  x_ref, k_ref, v_ref,
                  g_mix, win_hbm, cw4_ref, cb4_ref, wgate_ref, ba_ref, bx_ref, lam_ref,
                  cw31_ref, cb31_ref, lng_ref, lnb_ref, wout_hbm,
                  g_xa, wq_hbm, wo_hbm,
                  g_ffn, wup_hbm, cwf_ref, cbf_ref, wdown_hbm, g_final,
                  o_ref,
                  win_ref, wout_ref, wq_ref, wo_ref, wup_ref, wdown_ref, wsem,
                  x1buf, x2buf, xbuf, cbuf, cbbuf, gatebuf, hcar, abuf, ubuf, ycat, obuf, ghalo, gbuf, ubuf2, actbuf):
    s = pl.program_id(0)
    n_tiles = pl.num_programs(0) - 1
    mix_tile = jnp.minimum(s, n_tiles - 1)
    out_tile = jnp.maximum(s - 1, 0)

    @pl.when(s == 0)
    def _():
        _load_weights_as_bf16([(win_hbm, win_ref), (wq_hbm, wq_ref), (wo_hbm, wo_ref), (wup_hbm, wup_ref),
                               (wdown_hbm, wdown_ref), (wout_hbm, wout_ref)], (x1buf, x2buf), wsem)
        x1buf[...] = jnp.zeros((TM, D_MODEL), _F32)

    @pl.when(out_tile % tiles_per_seq == 0)
    def _():
        ghalo[...] = jnp.zeros((SUBLANES, D_FF), _F32)

    _mix_carry_in(mix_tile % tiles_per_seq == 0, xbuf, cbuf, hcar)

    def step(with_mix, with_rest):
        state = {"next_row": 0, "carry": None}
        m0 = _mix_block(0, state, x_ref, g_mix, win_ref, cw4_ref, cb4_ref, wgate_ref, ba_ref, bx_ref, lam_ref,
                        cw31_ref, cb31_ref, lng_ref, lnb_ref, wout_ref, x1buf,
                        xbuf, cbuf, cbbuf, gatebuf, hcar, abuf, ubuf, ycat)
        xa = _xattn(x1buf, g_xa, wq_ref, k_ref, v_ref, wo_ref, x2buf, obuf, gbuf)
        ff = _ffn(x2buf, g_ffn, wup_ref, cwf_ref, cbf_ref, wdown_ref, g_final, o_ref, ghalo, gbuf, ubuf2, actbuf)
        schedule = [m0, xa,
                    xa, m0, m0, xa,
                    xa,
                    ff, ff, ff, ff, ff, ff,
                    ff, m0]
        stages = ((m0,) if with_mix else ()) + ((xa, ff) if with_rest else ())
        live = {id(g): True for g in stages}
        for g in schedule:
            if id(g) in live:
                live[id(g)] = next(g, _DONE) is not _DONE
        assert not any(live.values()), "schedule must exhaust every stage"

    step(True, True)


def _const_spec(shape):
    nd = len(shape)
    return pl.BlockSpec(shape, lambda s: (0,) * nd, pipeline_mode=pl.Buffered(1))


def _block_diag_gates(w_a, w_x):
    heads_per_half = HALF // LRU_HD
    eye = jnp.eye(heads_per_half, dtype=w_a.dtype)

    def bd(w):
        w = w.reshape(D_LRU // HALF, heads_per_half, LRU_HD, LRU_HD)
        return jnp.einsum("jhik,hg->jhigk", w, eye).reshape(D_LRU // HALF, HALF, HALF)

    return jnp.concatenate([bd(w_a), bd(w_x)], axis=-1)


def kernel(x, mem, mix_norm_g, w_in, lru_conv_w, lru_conv_b, lru_w_a, lru_b_a, lru_w_x, lru_b_x, lru_lambda, conf_conv_w, conf_conv_b, conf_ln_g, conf_ln_b, w_out, xa_norm_g, mem_norm_g, w_q, w_kv, w_o, ffn_norm_g, w_up, ffn_conv_w, ffn_conv_b, w_down, final_norm_g):
    bsz, seq, d = x.shape
    assert d == D_MODEL and seq % TM == 0 and mem.shape == (bsz, N_MEM, D_MODEL)
    assert w_in.shape[0] == 1
    tiles_per_seq = seq // TM
    n_tiles = bsz * tiles_per_seq
    row = lambda v: v.reshape(1, -1)

    k, v = pl.pallas_call(
        _kv_kernel,
        grid=(bsz,),
        in_specs=[pl.BlockSpec((None, N_MEM, D_MODEL), lambda b: (b, 0, 0)),
                  pl.BlockSpec((1, D_MODEL), lambda b: (0, 0)),
                  pl.BlockSpec((None, D_MODEL, 2 * D_MODEL), lambda b: (0, 0, 0), pipeline_mode=pl.Buffered(1))],
        out_specs=[pl.BlockSpec((None, N_MEM, D_MODEL), lambda b: (b, 0, 0))] * 2,
        out_shape=[jax.ShapeDtypeStruct((bsz, N_MEM, D_MODEL), _BF16)] * 2,
        compiler_params=pltpu.CompilerParams(dimension_semantics=("arbitrary",),
                                             vmem_limit_bytes=VMEM_LIMIT_KV),
        name="kv_proj",
    )(mem, row(mem_norm_g[0]), w_kv)

    def mix_rows(s):
        j = jnp.minimum(s, n_tiles - 1)
        return (j // tiles_per_seq, j % tiles_per_seq, 0)

    def out_rows(s):
        j = jnp.maximum(s - 1, 0)
        return (j // tiles_per_seq, j % tiles_per_seq, 0)

    kv_spec = pl.BlockSpec((None, N_MEM, D_MODEL), lambda s: (jnp.maximum(s - 1, 0) // tiles_per_seq, 0, 0),
                           pipeline_mode=pl.Buffered(1))
    f32_scratch = lambda rows, cols: pltpu.VMEM((rows, cols), _F32)
    bf16_scratch = lambda rows, cols: pltpu.VMEM((rows, cols), _BF16)
    hbm = pl.BlockSpec(memory_space=pl.ANY)
    return pl.pallas_call(
        functools.partial(_layer_kernel, tiles_per_seq),
        grid=(n_tiles + 1,),
        in_specs=[pl.BlockSpec((None, TM, D_MODEL), mix_rows), kv_spec, kv_spec,
                  _const_spec((1, D_MODEL)),
                  hbm,
                  _const_spec((LRU_CONV, D_LRU)),
                  _const_spec((1, D_LRU)),
                  _const_spec((D_LRU // HALF, HALF, 2 * HALF)),
                  _const_spec((1, D_LRU)),
                  _const_spec((1, D_LRU)),
                  _const_spec((1, D_LRU)),
                  _const_spec((CONF_CONV, D_CONF)),
                  _const_spec((1, D_CONF)),
                  _const_spec((1, D_CONF)),
                  _const_spec((1, D_CONF)),
                  hbm,
                  _const_spec((1, D_MODEL)),
                  hbm, hbm,
                  _const_spec((1, D_MODEL)),
                  hbm,
                  _const_spec((FFN_CONV, D_FF)),
                  _const_spec((1, D_FF)),
                  hbm,
                  _const_spec((1, D_MODEL))],
        out_specs=pl.BlockSpec((None, TM, D_MODEL), out_rows),
        out_shape=jax.ShapeDtypeStruct(x.shape, x.dtype),
        scratch_shapes=[bf16_scratch(D_MODEL, 2 * (D_LRU + D_CONF)),
                        bf16_scratch(D_LRU + D_CONF, D_MODEL),
                        bf16_scratch(D_MODEL, D_MODEL),
                        bf16_scratch(D_MODEL, D_MODEL),
                        bf16_scratch(D_MODEL, 2 * D_FF),
                        bf16_scratch(D_FF, D_MODEL),
                        pltpu.SemaphoreType.DMA((2,)),
                        f32_scratch(TM, D_MODEL),
                        f32_scratch(TM, D_MODEL),
                        f32_scratch(SUBLANES + TM, D_LRU),
                        f32_scratch(CONF_HALO + TM, D_CONF),
                        f32_scratch(TM, D_CONF),
                        f32_scratch(TM, D_LRU),
                        f32_scratch(SUBLANES, D_LRU),
                        f32_scratch(MIX_ROWS, D_LRU),
                        f32_scratch(MIX_ROWS, D_LRU),
                        bf16_scratch(TM, D_LRU + D_CONF),
                        bf16_scratch(TM, D_MODEL),
                        f32_scratch(SUBLANES, D_FF),
                        pltpu.VMEM((2, SUBLANES + TM, FF_CHUNK), _F32),
                        f32_scratch(TM, FF_CHUNK),
                        bf16_scratch(TM, D_FF)],
        compiler_params=pltpu.CompilerParams(dimension_semantics=("arbitrary",),
                                             vmem_limit_bytes=VMEM_LIMIT_LAYER),
        name="layer",
    )(x, k, v,
      row(mix_norm_g[0]), w_in, lru_conv_w[0], row(lru_conv_b[0]),
      _block_diag_gates(lru_w_a[0], lru_w_x[0]).astype(_BF16), row(lru_b_a[0]), row(lru_b_x[0]),
      row(lru_lambda[0]), conf_conv_w[0], row(conf_conv_b[0]), row(conf_ln_g[0]), row(conf_ln_b[0]), w_out,
      row(xa_norm_g[0]), w_q, w_o,
      row(ffn_norm_g[0]), w_up, ffn_conv_w[0], row(ffn_conv_b[0]), w_down, row(final_norm_g))
```

```python
import functools
import math

import jax
import jax.numpy as jnp
from jax import lax
from jax.experimental import pallas as pl
from jax.experimental.pallas import tpu as pltpu

D_MODEL = 1024
N_MEM = 256
D_LRU = 512
D_CONF = 512
LRU_HD = 64
LRU_CONV = 4
RG_C = 8.0
CONF_CONV = 31
XA_HEADS = 4
XA_HD = 256
D_FF = 3072
FFN_CONV = 3
EPS = 1e-6

SUBLANES = 8
HALF = 256
TM = 512
MIX_ROWS = TM
CONV_ROWS = TM
FF_CHUNK = 1024
CONF_HALO = 32
VMEM_LIMIT_KV = 32 * 1024 * 1024
VMEM_LIMIT_LAYER = 63 * 1024 * 1024

_BF16 = jnp.bfloat16
_F32 = jnp.float32
_DONE = object()


def _rms(x, g):
    return x * lax.rsqrt(jnp.mean(x * x, axis=-1, keepdims=True) + EPS) * g


_LOG2E = math.log2(math.e)


def _gelu_tanh_times(x, y):
    c = math.sqrt(2.0 / math.pi)
    k1 = -2.0 * c * _LOG2E
    k3 = k1 * 0.044715
    e = jnp.exp2(x * (k1 + k3 * (x * x)))
    return (x * y) / (1.0 + e)


def _sigmoid(x):
    return 1.0 / (1.0 + jnp.exp2(x * (-_LOG2E)))


def _dot(a, b):
    return jnp.dot(a, b, preferred_element_type=_F32)


def _shift_rows_1(x):
    return pltpu.roll(x, 1, 0)


def _kv_kernel(mem_ref, g_ref, wkv_ref, k_ref, v_ref):
    m = _rms(mem_ref[...], g_ref[...]).astype(_BF16)
    kv = _dot(m, wkv_ref[...].astype(_BF16))
    k_ref[...] = (kv[:, :D_MODEL] * (XA_HD ** -0.5)).astype(_BF16)
    v_ref[...] = kv[:, D_MODEL:].astype(_BF16)


def _mix_carry_in(first_tile, xbuf, cbuf, hcar):
    @pl.when(first_tile)
    def _():
        xbuf[0:SUBLANES, :] = jnp.zeros((SUBLANES, D_LRU), _F32)
        cbuf[0:CONF_HALO, :] = jnp.zeros((CONF_HALO, D_CONF), _F32)
        hcar[...] = jnp.zeros((SUBLANES, D_LRU), _F32)

    @pl.when(jnp.logical_not(first_tile))
    def _():
        xbuf[0:SUBLANES, :] = xbuf[TM:TM + SUBLANES, :]
        cbuf[0:CONF_HALO, :] = cbuf[TM:TM + CONF_HALO, :]


def _conv4(r0, cw4_ref, cb4_ref, xbuf):
    ext = xbuf[r0:r0 + MIX_ROWS + SUBLANES, :]
    acc = cw4_ref[0:1, :] * ext
    for k in range(1, LRU_CONV):
        acc = _shift_rows_1(acc) + cw4_ref[k:k + 1, :] * ext
    return acc[SUBLANES:, :] + cb4_ref[...]


def _conv31(r0, n_rows, cw31_ref, cb31_ref, cbuf, c4buf):
    ext_rows = n_rows + SUBLANES
    win_rows = CONF_HALO + n_rows
    c4buf[0:win_rows, :] = pltpu.roll(cbuf[r0:r0 + win_rows, :], 4, 0)
    parts = [[None, None], [None, None]]
    for m in range(2):
        for r in range(2):
            for q in range(CONF_HALO // SUBLANES):
                for s in range(2):
                    d = 8 * q + 4 * s + 2 * m + r
                    if d >= CONF_CONV:
                        continue
                    k = CONF_CONV - 1 - d
                    start = CONF_HALO - SUBLANES - SUBLANES * q
                    src = c4buf[start:start + ext_rows, :] if s else cbuf[r0 + start:r0 + start + ext_rows, :]
                    term = cw31_ref[k:k + 1, :] * src
                    parts[m][r] = term if parts[m][r] is None else parts[m][r] + term
    even = parts[0][0] + _shift_rows_1(parts[0][1])
    odd = parts[1][0] + _shift_rows_1(parts[1][1])
    acc = even + pltpu.roll(odd, 2, 0)
    return acc[SUBLANES:, :] + cb31_ref[...]


def _recurrence(a, u, carry, abuf, ubuf):
    groups = MIX_ROWS // SUBLANES
    a3 = a.reshape(groups, SUBLANES, D_LRU)
    u3 = u.reshape(groups, SUBLANES, D_LRU)
    sub = lax.broadcasted_iota(jnp.int32, (groups, SUBLANES, D_LRU), 1)
    for k in (1, 2, 4):
        keep = sub >= k
        a_prev = jnp.where(keep, pltpu.roll(a3, k, 1), 1.0)
        u_prev = jnp.where(keep, pltpu.roll(u3, k, 1), 0.0)
        u3 = u3 + a3 * u_prev
        a3 = a3 * a_prev
    abuf[...] = a3.reshape(MIX_ROWS, D_LRU)
    ubuf[...] = u3.reshape(MIX_ROWS, D_LRU)
    for gidx in range(groups):
        rows = pl.ds(gidx * SUBLANES, SUBLANES)
        hg = ubuf[rows, :] + abuf[rows, :] * carry
        ubuf[rows, :] = hg
        carry = jnp.broadcast_to(hg[SUBLANES - 1:SUBLANES, :], (SUBLANES, D_LRU))
    return ubuf[...], carry


def _mix_block(r0, state, x_ref, g_ref, win_ref, cw4_ref, cb4_ref, wgate_ref, ba_ref, bx_ref, lam_ref,
               cw31_ref, cb31_ref, lng_ref, lnb_ref, wout_ref, o_ref,
               xbuf, cbuf, c4buf, cbbuf, gatebuf, hcar, abuf, ubuf, ycat):
    rows = slice(r0, r0 + MIX_ROWS)
    h = _rms(x_ref[rows, :], g_ref[...]).astype(_BF16)
    z = _dot(h, win_ref[:, 0:2 * D_LRU])
    xbuf[SUBLANES + r0:SUBLANES + r0 + MIX_ROWS, :] = z[:, 0:D_LRU]
    gatebuf[rows, :] = z[:, D_LRU:]
    yield

    xc = _conv4(r0, cw4_ref, cb4_ref, xbuf)
    xcb = xc.astype(_BF16)
    for j in range(D_LRU // HALF):
        gj = _dot(xcb[:, j * HALF:(j + 1) * HALF], wgate_ref[j])
        abuf[:, j * HALF:(j + 1) * HALF] = gj[:, :HALF]
        ubuf[:, j * HALF:(j + 1) * HALF] = gj[:, HALF:]
    r = _sigmoid(abuf[...] + ba_ref[...])
    i = _sigmoid(ubuf[...] + bx_ref[...])
    nlam = -lam_ref[...]
    softplus = jnp.maximum(nlam, 0.0) + jnp.log1p(jnp.exp(-jnp.abs(nlam)))
    log_a = (-RG_C * softplus) * r
    a = jnp.exp(log_a)
    u = jnp.sqrt(-jnp.tanh(log_a) * (a * a + 1.0)) * (i * xc)
    assert state["next_row"] == r0, "row blocks must run their second piece in order"
    carry = hcar[...] if r0 == 0 else state["carry"]
    hseq, carry = _recurrence(a, u, carry, abuf, ubuf)
    state.update(next_row=r0 + MIX_ROWS, carry=carry)
    if r0 + MIX_ROWS == TM:
        hcar[...] = carry
    ycat[rows, 0:D_LRU] = _gelu_tanh_times(gatebuf[rows, :], hseq).astype(_BF16)

    for c0 in range(r0, r0 + MIX_ROWS, CONV_ROWS):
        yield
        crows = slice(c0, c0 + CONV_ROWS)
        brows = slice(CONF_HALO + c0, CONF_HALO + c0 + CONV_ROWS)
        zc = _dot(h[c0 - r0:c0 - r0 + CONV_ROWS, :], win_ref[:, 2 * D_LRU:])
        cbuf[brows, :] = zc[:, 0:D_CONF]
        cbbuf[crows, :] = zc[:, D_CONF:]
        cbuf[brows, :] = cbuf[brows, :] * _sigmoid(cbbuf[crows, :])
        c = _conv31(c0, CONV_ROWS, cw31_ref, cb31_ref, cbuf, c4buf)
        mu = jnp.mean(c, axis=-1, keepdims=True)
        cc = c - mu
        var = jnp.mean(cc * cc, axis=-1, keepdims=True)
        cn = cc * lax.rsqrt(var + EPS) * lng_ref[...] + lnb_ref[...]
        ycat[crows, D_LRU:] = (cn * _sigmoid(cn)).astype(_BF16)
    yield
    o_ref[rows, :] = _dot(ycat[rows, :], wout_ref[...])
    o_ref[rows, :] = o_ref[rows, :] + x_ref[rows, :]


def _xattn(x_ref, g_ref, wq_ref, k_ref, v_ref, wo_ref, o_ref, obuf, sbuf):
    h = _rms(x_ref[...], g_ref[...]).astype(_BF16)
    o_ref[...] = _dot(h, wq_ref[...])
    q = o_ref[...].astype(_BF16)
    yield
    heads = [slice(hd * XA_HD, (hd + 1) * XA_HD) for hd in range(XA_HEADS)]
    for hd, cols in enumerate(heads):
        sbuf[hd // 2, 0:TM, heads[hd % 2]] = lax.dot_general(
            q[:, cols], k_ref[:, cols], (((1,), (1,)), ((), ())), preferred_element_type=_F32)
    yield
    for hd, cols in enumerate(heads):
        s = sbuf[hd // 2, 0:TM, heads[hd % 2]]
        e = jnp.exp(s - jnp.max(s, axis=-1, keepdims=True))
        p = e * (1.0 / jnp.sum(e, axis=-1, keepdims=True))
        obuf[:, cols] = _dot(p.astype(_BF16), v_ref[:, cols]).astype(_BF16)
    yield
    o_ref[...] = _dot(obuf[...], wo_ref[...])
    o_ref[...] = o_ref[...] + x_ref[...]


def _ffn(x_ref, g_ref, wup_ref, cw_ref, cb_ref, wdown_ref, gf_ref, o_ref, ghalo, gbuf, ubuf2, actbuf):
    h = _rms(x_ref[...], g_ref[...]).astype(_BF16)
    for idx, c0 in enumerate(range(0, D_FF, FF_CHUNK)):
        cols = slice(c0, c0 + FF_CHUNK)
        slot = idx % 2
        gbuf[slot, 0:SUBLANES, :] = ghalo[:, cols]
        gbuf[slot, SUBLANES:SUBLANES + TM, :] = _dot(h, wup_ref[:, cols])
        ubuf2[...] = _dot(h, wup_ref[:, D_FF + c0:D_FF + c0 + FF_CHUNK])
        ghalo[:, cols] = gbuf[slot, TM:TM + SUBLANES, :]
        ext = gbuf[slot]
        acc = cw_ref[0:1, cols] * ext
        for k in range(1, FFN_CONV):
            acc = _shift_rows_1(acc) + cw_ref[k:k + 1, cols] * ext
        gc = acc[SUBLANES:, :] + cb_ref[:, cols]
        actbuf[:, cols] = _gelu_tanh_times(gc, ubuf2[...]).astype(_BF16)
        yield
    o_ref[...] = _dot(actbuf[...], wdown_ref[...])
    o_ref[...] = _rms(x_ref[...] + o_ref[...], gf_ref[...])


def _load_weights_as_bf16(weights, stages, sems):
    chunks = [(src, dst, r, c)
              for src, dst in weights
              for r in range(0, dst.shape[0], TM)
              for c in range(0, dst.shape[1], D_MODEL)]

    def chunk_copy(i):
        src, _, r, c = chunks[i]
        return pltpu.make_async_copy(src.at[0, pl.ds(r, TM), pl.ds(c, D_MODEL)], stages[i % 2], sems.at[i % 2])

    chunk_copy(0).start()
    for i, (_, dst, r, c) in enumerate(chunks):
        if i + 1 < len(chunks):
            chunk_copy(i + 1).start()
        chunk_copy(i).wait()
        dst[r:r + TM, c:c + D_MODEL] = stages[i % 2][...].astype(_BF16)


def _layer_kernel(tiles_per_seq,
                  x_ref, k_ref, v_ref,
                  g_mix, win_hbm, cw4_ref, cb4_ref, wgate_ref, ba_ref, bx_ref, lam_ref,
                  cw31_ref, cb31_ref, lng_ref, lnb_ref, wout_hbm,
                  g_xa, wq_hbm, wo_hbm,
                  g_ffn, wup_hbm, cwf_ref, cbf_ref, wdown_hbm, g_final,
                  o_ref,
                  win_ref, wout_ref, wq_ref, wo_ref, wup_ref, wdown_ref, wsem,
                  x1buf, x2buf, xbuf, cbuf, c4buf, cbbuf, gatebuf, hcar, abuf, ubuf, ycat, obuf, ghalo, gbuf, ubuf2,
                  actbuf):
    s = pl.program_id(0)
    n_tiles = pl.num_programs(0) - 1
    mix_tile = jnp.minimum(s, n_tiles - 1)
    out_tile = jnp.maximum(s - 1, 0)

    @pl.when(s == 0)
    def _():
        _load_weights_as_bf16([(win_hbm, win_ref), (wq_hbm, wq_ref), (wo_hbm, wo_ref), (wup_hbm, wup_ref),
                               (wdown_hbm, wdown_ref), (wout_hbm, wout_ref)], (x1buf, x2buf), wsem)
        x1buf[...] = jnp.zeros((TM, D_MODEL), _F32)

    @pl.when(out_tile % tiles_per_seq == 0)
    def _():
        ghalo[...] = jnp.zeros((SUBLANES, D_FF), _F32)

    _mix_carry_in(mix_tile % tiles_per_seq == 0, xbuf, cbuf, hcar)

    state = {"next_row": 0, "carry": None}
    m0 = _mix_block(0, state, x_ref, g_mix, win_ref, cw4_ref, cb4_ref, wgate_ref, ba_ref, bx_ref, lam_ref,
                    cw31_ref, cb31_ref, lng_ref, lnb_ref, wout_ref, x1buf,
                    xbuf, cbuf, c4buf, cbbuf, gatebuf, hcar, abuf, ubuf, ycat)
    xa = _xattn(x1buf, g_xa, wq_ref, k_ref, v_ref, wo_ref, x2buf, obuf, gbuf)
    ff = _ffn(x2buf, g_ffn, wup_ref, cwf_ref, cbf_ref, wdown_ref, g_final, o_ref, ghalo, gbuf, ubuf2, actbuf)
    schedule = [m0, xa,
                xa, m0, m0, xa,
                xa,
                *[ff] * (D_FF // FF_CHUNK),
                ff, m0]
    live = {id(g): True for g in (m0, xa, ff)}
    for g in schedule:
        live[id(g)] = next(g, _DONE) is not _DONE
    assert not any(live.values()), "schedule must exhaust every stage"


def _const_spec(shape):
    nd = len(shape)
    return pl.BlockSpec(shape, lambda s: (0,) * nd, pipeline_mode=pl.Buffered(1))


def _block_diag_gates(w_a, w_x):
    heads_per_half = HALF // LRU_HD
    eye = jnp.eye(heads_per_half, dtype=w_a.dtype)

    def bd(w):
        w = w.reshape(D_LRU // HALF, heads_per_half, LRU_HD, LRU_HD)
        return jnp.einsum("jhik,hg->jhigk", w, eye).reshape(D_LRU // HALF, HALF, HALF)

    return jnp.concatenate([bd(w_a), bd(w_x)], axis=-1)


def kernel(x, mem, mix_norm_g, w_in, lru_conv_w, lru_conv_b, lru_w_a, lru_b_a, lru_w_x, lru_b_x, lru_lambda, conf_conv_w, conf_conv_b, conf_ln_g, conf_ln_b, w_out, xa_norm_g, mem_norm_g, w_q, w_kv, w_o, ffn_norm_g, w_up, ffn_conv_w, ffn_conv_b, w_down, final_norm_g):
    bsz, seq, d = x.shape
    assert d == D_MODEL and seq % TM == 0 and mem.shape == (bsz, N_MEM, D_MODEL)
    assert w_in.shape[0] == 1
    tiles_per_seq = seq // TM
    n_tiles = bsz * tiles_per_seq
    row = lambda v: v.reshape(1, -1)

    k, v = pl.pallas_call(
        _kv_kernel,
        grid=(bsz,),
        in_specs=[pl.BlockSpec((None, N_MEM, D_MODEL), lambda b: (b, 0, 0)),
                  pl.BlockSpec((1, D_MODEL), lambda b: (0, 0)),
                  pl.BlockSpec((None, D_MODEL, 2 * D_MODEL), lambda b: (0, 0, 0), pipeline_mode=pl.Buffered(1))],
        out_specs=[pl.BlockSpec((None, N_MEM, D_MODEL), lambda b: (b, 0, 0))] * 2,
        out_shape=[jax.ShapeDtypeStruct((bsz, N_MEM, D_MODEL), _BF16)] * 2,
        compiler_params=pltpu.CompilerParams(dimension_semantics=("arbitrary",),
                                             vmem_limit_bytes=VMEM_LIMIT_KV),
        name="kv_proj",
    )(mem, row(mem_norm_g[0]), w_kv)

    def mix_rows(s):
        j = jnp.minimum(s, n_tiles - 1)
        return (j // tiles_per_seq, j % tiles_per_seq, 0)

    def out_rows(s):
        j = jnp.maximum(s - 1, 0)
        return (j // tiles_per_seq, j % tiles_per_seq, 0)

    kv_spec = pl.BlockSpec((None, N_MEM, D_MODEL), lambda s: (jnp.maximum(s - 1, 0) // tiles_per_seq, 0, 0),
                           pipeline_mode=pl.Buffered(1))
    f32_scratch = lambda rows, cols: pltpu.VMEM((rows, cols), _F32)
    bf16_scratch = lambda rows, cols: pltpu.VMEM((rows, cols), _BF16)
    hbm = pl.BlockSpec(memory_space=pl.ANY)
    return pl.pallas_call(
        functools.partial(_layer_kernel, tiles_per_seq),
        grid=(n_tiles + 1,),
        in_specs=[pl.BlockSpec((None, TM, D_MODEL), mix_rows), kv_spec, kv_spec,
                  _const_spec((1, D_MODEL)),
                  hbm,
                  _const_spec((LRU_CONV, D_LRU)),
                  _const_spec((1, D_LRU)),
                  _const_spec((D_LRU // HALF, HALF, 2 * HALF)),
                  _const_spec((1, D_LRU)),
                  _const_spec((1, D_LRU)),
                  _const_spec((1, D_LRU)),
                  _const_spec((CONF_CONV, D_CONF)),
                  _const_spec((1, D_CONF)),
                  _const_spec((1, D_CONF)),
                  _const_spec((1, D_CONF)),
                  hbm,
                  _const_spec((1, D_MODEL)),
                  hbm, hbm,
                  _const_spec((1, D_MODEL)),
                  hbm,
                  _const_spec((FFN_CONV, D_FF)),
                  _const_spec((1, D_FF)),
                  hbm,
                  _const_spec((1, D_MODEL))],
        out_specs=pl.BlockSpec((None, TM, D_MODEL), out_rows),
        out_shape=jax.ShapeDtypeStruct(x.shape, x.dtype),
        scratch_shapes=[bf16_scratch(D_MODEL, 2 * (D_LRU + D_CONF)),
                        bf16_scratch(D_LRU + D_CONF, D_MODEL),
                        bf16_scratch(D_MODEL, D_MODEL),
                        bf16_scratch(D_MODEL, D_MODEL),
                        bf16_scratch(D_MODEL, 2 * D_FF),
                        bf16_scratch(D_FF, D_MODEL),
                        pltpu.SemaphoreType.DMA((2,)),
                        f32_scratch(TM, D_MODEL),
                        f32_scratch(TM, D_MODEL),
                        f32_scratch(SUBLANES + TM, D_LRU),
                        f32_scratch(CONF_HALO + TM, D_CONF),
                        f32_scratch(CONF_HALO + TM, D_CONF),
                        f32_scratch(TM, D_CONF),
                        f32_scratch(TM, D_LRU),
                        f32_scratch(SUBLANES, D_LRU),
                        f32_scratch(MIX_ROWS, D_LRU),
                        f32_scratch(MIX_ROWS, D_LRU),
                        bf16_scratch(TM, D_LRU + D_CONF),
                        bf16_scratch(TM, D_MODEL),
                        f32_scratch(SUBLANES, D_FF),
                        pltpu.VMEM((2, SUBLANES + TM, FF_CHUNK), _F32),
                        f32_scratch(TM, FF_CHUNK),
                        bf16_scratch(TM, D_FF)],
        compiler_params=pltpu.CompilerParams(dimension_semantics=("arbitrary",),
                                             vmem_limit_bytes=VMEM_LIMIT_LAYER),
        name="layer",
    )(x, k, v,
      row(mix_norm_g[0]), w_in, lru_conv_w[0], row(lru_conv_b[0]),
      _block_diag_gates(lru_w_a[0], lru_w_x[0]).astype(_BF16), row(lru_b_a[0]), row(lru_b_x[0]),
      row(lru_lambda[0]), conf_conv_w[0], row(conf_conv_b[0]), row(conf_ln_g[0]), row(conf_ln_b[0]), w_out,
      row(xa_norm_g[0]), w_q, w_o,
      row(ffn_norm_g[0]), w_up, ffn_conv_w[0], row(ffn_conv_b[0]), w_down, row(final_norm_g))
```

```python
import functools
import math

import jax
import jax.numpy as jnp
from jax import lax
from jax.experimental import pallas as pl
from jax.experimental.pallas import tpu as pltpu

D_MODEL = 1024
N_MEM = 256
D_LRU = 512
D_CONF = 512
LRU_HD = 64
LRU_CONV = 4
RG_C = 8.0
CONF_CONV = 31
XA_HEADS = 4
XA_HD = 256
D_FF = 3072
FFN_CONV = 3
EPS = 1e-6

SUBLANES = 8
HALF = 256
TM = 512
MIX_ROWS = TM
CONV_ROWS = TM
FF_CHUNK = 512
CONF_HALO = 32
VMEM_LIMIT_KV = 32 * 1024 * 1024
VMEM_LIMIT_LAYER = 63 * 1024 * 1024

_BF16 = jnp.bfloat16
_F32 = jnp.float32
_DONE = object()


def _rms(x, g):
    return x * lax.rsqrt(jnp.mean(x * x, axis=-1, keepdims=True) + EPS) * g


_LOG2E = math.log2(math.e)


def _gelu_tanh_times(x, y):
    c = math.sqrt(2.0 / math.pi)
    k1 = -2.0 * c * _LOG2E
    k3 = k1 * 0.044715
    e = jnp.exp2(x * (k1 + k3 * (x * x)))
    return (x * y) / (1.0 + e)


def _sigmoid(x):
    return 1.0 / (1.0 + jnp.exp2(x * (-_LOG2E)))


def _dot(a, b):
    return jnp.dot(a, b, preferred_element_type=_F32)


def _shift_rows_1(x):
    return pltpu.roll(x, 1, 0)


def _kv_kernel(mem_ref, g_ref, wkv_ref, k_ref, v_ref):
    m = _rms(mem_ref[...], g_ref[...]).astype(_BF16)
    kv = _dot(m, wkv_ref[...].astype(_BF16))
    k_ref[...] = (kv[:, :D_MODEL] * (XA_HD ** -0.5)).astype(_BF16)
    v_ref[...] = kv[:, D_MODEL:].astype(_BF16)


def _mix_carry_in(first_tile, xbuf, cbuf, hcar):
    @pl.when(first_tile)
    def _():
        xbuf[0:SUBLANES, :] = jnp.zeros((SUBLANES, D_LRU), _F32)
        cbuf[0:CONF_HALO, :] = jnp.zeros((CONF_HALO, D_CONF), _F32)
        hcar[...] = jnp.zeros((SUBLANES, D_LRU), _F32)

    @pl.when(jnp.logical_not(first_tile))
    def _():
        xbuf[0:SUBLANES, :] = xbuf[TM:TM + SUBLANES, :]
        cbuf[0:CONF_HALO, :] = cbuf[TM:TM + CONF_HALO, :]


def _conv4(r0, cw4_ref, cb4_ref, xbuf):
    ext = xbuf[r0:r0 + MIX_ROWS + SUBLANES, :]
    acc = cw4_ref[0:1, :] * ext
    for k in range(1, LRU_CONV):
        acc = _shift_rows_1(acc) + cw4_ref[k:k + 1, :] * ext
    return acc[SUBLANES:, :] + cb4_ref[...]


def _conv31(r0, n_rows, cw31_ref, cb31_ref, cbuf):
    ext_rows = n_rows + SUBLANES
    win = cbuf[r0:r0 + CONF_HALO + n_rows, :]
    sources = (win, pltpu.roll(win, 4, 0))
    parts = [[None, None], [None, None]]
    for m in range(2):
        for r in range(2):
            for q in range(CONF_HALO // SUBLANES):
                for s in range(2):
                    d = 8 * q + 4 * s + 2 * m + r
                    if d >= CONF_CONV:
                        continue
                    k = CONF_CONV - 1 - d
                    start = CONF_HALO - SUBLANES - SUBLANES * q
                    term = cw31_ref[k:k + 1, :] * sources[s][start:start + ext_rows, :]
                    parts[m][r] = term if parts[m][r] is None else parts[m][r] + term
    even = parts[0][0] + _shift_rows_1(parts[0][1])
    odd = parts[1][0] + _shift_rows_1(parts[1][1])
    acc = even + pltpu.roll(odd, 2, 0)
    return acc[SUBLANES:, :] + cb31_ref[...]


def _recurrence(a, u, carry, abuf, ubuf):
    groups = MIX_ROWS // SUBLANES
    a3 = a.reshape(groups, SUBLANES, D_LRU)
    u3 = u.reshape(groups, SUBLANES, D_LRU)
    sub = lax.broadcasted_iota(jnp.int32, (groups, SUBLANES, D_LRU), 1)
    for k in (1, 2, 4):
        keep = sub >= k
        a_prev = jnp.where(keep, pltpu.roll(a3, k, 1), 1.0)
        u_prev = jnp.where(keep, pltpu.roll(u3, k, 1), 0.0)
        u3 = u3 + a3 * u_prev
        a3 = a3 * a_prev
    abuf[...] = a3.reshape(MIX_ROWS, D_LRU)
    ubuf[...] = u3.reshape(MIX_ROWS, D_LRU)
    for gidx in range(groups):
        rows = pl.ds(gidx * SUBLANES, SUBLANES)
        hg = ubuf[rows, :] + abuf[rows, :] * carry
        ubuf[rows, :] = hg
        carry = jnp.broadcast_to(hg[SUBLANES - 1:SUBLANES, :], (SUBLANES, D_LRU))
    return ubuf[...], carry


def _mix_block(r0, state, x_ref, g_ref, win_ref, cw4_ref, cb4_ref, wgate_ref, ba_ref, bx_ref, lam_ref,
               cw31_ref, cb31_ref, lng_ref, lnb_ref, wout_ref, o_ref,
               xbuf, cbuf, cbbuf, gatebuf, hcar, abuf, ubuf, ycat):
    rows = slice(r0, r0 + MIX_ROWS)
    h = _rms(x_ref[rows, :], g_ref[...]).astype(_BF16)
    z = _dot(h, win_ref[:, 0:2 * D_LRU])
    xbuf[SUBLANES + r0:SUBLANES + r0 + MIX_ROWS, :] = z[:, 0:D_LRU]
    gatebuf[rows, :] = z[:, D_LRU:]
    yield

    xc = _conv4(r0, cw4_ref, cb4_ref, xbuf)
    xcb = xc.astype(_BF16)
    for j in range(D_LRU // HALF):
        gj = _dot(xcb[:, j * HALF:(j + 1) * HALF], wgate_ref[j])
        abuf[:, j * HALF:(j + 1) * HALF] = gj[:, :HALF]
        ubuf[:, j * HALF:(j + 1) * HALF] = gj[:, HALF:]
    r = _sigmoid(abuf[...] + ba_ref[...])
    i = _sigmoid(ubuf[...] + bx_ref[...])
    nlam = -lam_ref[...]
    softplus = jnp.maximum(nlam, 0.0) + jnp.log1p(jnp.exp(-jnp.abs(nlam)))
    log_a = (-RG_C * softplus) * r
    a = jnp.exp(log_a)
    u = jnp.sqrt(-jnp.tanh(log_a) * (a * a + 1.0)) * (i * xc)
    assert state["next_row"] == r0, "row blocks must run their second piece in order"
    carry = hcar[...] if r0 == 0 else state["carry"]
    hseq, carry = _recurrence(a, u, carry, abuf, ubuf)
    state.update(next_row=r0 + MIX_ROWS, carry=carry)
    if r0 + MIX_ROWS == TM:
        hcar[...] = carry
    ycat[rows, 0:D_LRU] = _gelu_tanh_times(gatebuf[rows, :], hseq).astype(_BF16)

    for c0 in range(r0, r0 + MIX_ROWS, CONV_ROWS):
        yield
        crows = slice(c0, c0 + CONV_ROWS)
        brows = slice(CONF_HALO + c0, CONF_HALO + c0 + CONV_ROWS)
        zc = _dot(h[c0 - r0:c0 - r0 + CONV_ROWS, :], win_ref[:, 2 * D_LRU:])
        cbuf[brows, :] = zc[:, 0:D_CONF]
        cbbuf[crows, :] = zc[:, D_CONF:]
        cbuf[brows, :] = cbuf[brows, :] * _sigmoid(cbbuf[crows, :])
        c = _conv31(c0, CONV_ROWS, cw31_ref, cb31_ref, cbuf)
        mu = jnp.mean(c, axis=-1, keepdims=True)
        cc = c - mu
        var = jnp.mean(cc * cc, axis=-1, keepdims=True)
        cn = cc * lax.rsqrt(var + EPS) * lng_ref[...] + lnb_ref[...]
        ycat[crows, D_LRU:] = (cn * _sigmoid(cn)).astype(_BF16)
    yield
    o_ref[rows, :] = _dot(ycat[rows, :], wout_ref[...])
    o_ref[rows, :] = o_ref[rows, :] + x_ref[rows, :]


def _xattn(x_ref, g_ref, wq_ref, k_ref, v_ref, wo_ref, o_ref, obuf, sbuf):
    h = _rms(x_ref[...], g_ref[...]).astype(_BF16)
    o_ref[...] = _dot(h, wq_ref[...])
    q = o_ref[...].astype(_BF16)
    yield
    heads = [slice(hd * XA_HD, (hd + 1) * XA_HD) for hd in range(XA_HEADS)]
    for hd, cols in enumerate(heads):
        sbuf[hd // 2, 0:TM, heads[hd % 2]] = lax.dot_general(
            q[:, cols], k_ref[:, cols], (((1,), (1,)), ((), ())), preferred_element_type=_F32)
    yield
    for hd, cols in enumerate(heads):
        s = sbuf[hd // 2, 0:TM, heads[hd % 2]]
        e = jnp.exp(s - jnp.max(s, axis=-1, keepdims=True))
        p = e * (1.0 / jnp.sum(e, axis=-1, keepdims=True))
        obuf[:, cols] = _dot(p.astype(_BF16), v_ref[:, cols]).astype(_BF16)
    yield
    o_ref[...] = _dot(obuf[...], wo_ref[...])
    o_ref[...] = o_ref[...] + x_ref[...]


def _ffn(x_ref, g_ref, wup_ref, cw_ref, cb_ref, wdown_ref, gf_ref, o_ref, ghalo, gbuf, ubuf2, actbuf):
    h = _rms(x_ref[...], g_ref[...]).astype(_BF16)
    for idx, c0 in enumerate(range(0, D_FF, FF_CHUNK)):
        cols = slice(c0, c0 + FF_CHUNK)
        slot = idx % 2
        gbuf[slot, 0:SUBLANES, :] = ghalo[:, cols]
        gbuf[slot, SUBLANES:SUBLANES + TM, :] = _dot(h, wup_ref[:, cols])
        ubuf2[...] = _dot(h, wup_ref[:, D_FF + c0:D_FF + c0 + FF_CHUNK])
        ghalo[:, cols] = gbuf[slot, TM:TM + SUBLANES, :]
        ext = gbuf[slot]
        acc = cw_ref[0:1, cols] * ext
        for k in range(1, FFN_CONV):
            acc = _shift_rows_1(acc) + cw_ref[k:k + 1, cols] * ext
        gc = acc[SUBLANES:, :] + cb_ref[:, cols]
        actbuf[:, cols] = _gelu_tanh_times(gc, ubuf2[...]).astype(_BF16)
        yield
    o_ref[...] = _dot(actbuf[...], wdown_ref[...])
    o_ref[...] = _rms(x_ref[...] + o_ref[...], gf_ref[...])


def _load_weights_as_bf16(weights, stages, sems):
    chunks = [(src, dst, r, c)
              for src, dst in weights
              for r in range(0, dst.shape[0], TM)
              for c in range(0, dst.shape[1], D_MODEL)]

    def chunk_copy(i):
        src, _, r, c = chunks[i]
        return pltpu.make_async_copy(src.at[0, pl.ds(r, TM), pl.ds(c, D_MODEL)], stages[i % 2], sems.at[i % 2])

    chunk_copy(0).start()
    for i, (_, dst, r, c) in enumerate(chunks):
        if i + 1 < len(chunks):
            chunk_copy(i + 1).start()
        chunk_copy(i).wait()
        dst[r:r + TM, c:c + D_MODEL] = stages[i % 2][...].astype(_BF16)


def _layer_kernel(tiles_per_seq,
                  x_ref, k_ref, v_ref,
                  g_mix, win_hbm, cw4_ref, cb4_ref, wgate_ref, ba_ref, bx_ref, lam_ref,
                  cw31_ref, cb31_ref, lng_ref, lnb_ref, wout_hbm,
                  g_xa, wq_hbm, wo_hbm,
                  g_ffn, wup_hbm, cwf_ref, cbf_ref, wdown_hbm, g_final,
                  o_ref,
                  win_ref, wout_ref, wq_ref, wo_ref, wup_ref, wdown_ref, wsem,
                  x1buf, x2buf, xbuf, cbuf, cbbuf, gatebuf, hcar, abuf, ubuf, ycat, obuf, ghalo, gbuf, ubuf2, actbuf):
    s = pl.program_id(0)
    n_tiles = pl.num_programs(0) - 1
    mix_tile = jnp.minimum(s, n_tiles - 1)
    out_tile = jnp.maximum(s - 1, 0)

    @pl.when(s == 0)
    def _():
        _load_weights_as_bf16([(win_hbm, win_ref), (wq_hbm, wq_ref), (wo_hbm, wo_ref), (wup_hbm, wup_ref),
                               (wdown_hbm, wdown_ref), (wout_hbm, wout_ref)], (x1buf, x2buf), wsem)
        x1buf[...] = jnp.zeros((TM, D_MODEL), _F32)

    @pl.when(out_tile % tiles_per_seq == 0)
    def _():
        ghalo[...] = jnp.zeros((SUBLANES, D_FF), _F32)

    _mix_carry_in(mix_tile % tiles_per_seq == 0, xbuf, cbuf, hcar)

    state = {"next_row": 0, "carry": None}
    m0 = _mix_block(0, state, x_ref, g_mix, win_ref, cw4_ref, cb4_ref, wgate_ref, ba_ref, bx_ref, lam_ref,
                    cw31_ref, cb31_ref, lng_ref, lnb_ref, wout_ref, x1buf,
                    xbuf, cbuf, cbbuf, gatebuf, hcar, abuf, ubuf, ycat)
    xa = _xattn(x1buf, g_xa, wq_ref, k_ref, v_ref, wo_ref, x2buf, obuf, gbuf)
    ff = _ffn(x2buf, g_ffn, wup_ref, cwf_ref, cbf_ref, wdown_ref, g_final, o_ref, ghalo, gbuf, ubuf2, actbuf)
    schedule = [m0, xa,
                xa, m0, m0, xa,
                xa,
                *[ff] * (D_FF // FF_CHUNK),
                ff, m0]
    live = {id(g): True for g in (m0, xa, ff)}
    for g in schedule:
        live[id(g)] = next(g, _DONE) is not _DONE
    assert not any(live.values()), "schedule must exhaust every stage"


def _const_spec(shape):
    nd = len(shape)
    return pl.BlockSpec(shape, lambda s: (0,) * nd, pipeline_mode=pl.Buffered(1))


def _block_diag_gates(w_a, w_x):
    heads_per_half = HALF // LRU_HD
    eye = jnp.eye(heads_per_half, dtype=w_a.dtype)

    def bd(w):
        w = w.reshape(D_LRU // HALF, heads_per_half, LRU_HD, LRU_HD)
        return jnp.einsum("jhik,hg->jhigk", w, eye).reshape(D_LRU // HALF, HALF, HALF)

    return jnp.concatenate([bd(w_a), bd(w_x)], axis=-1)


def kernel(x, mem, mix_norm_g, w_in, lru_conv_w, lru_conv_b, lru_w_a, lru_b_a, lru_w_x, lru_b_x, lru_lambda, conf_conv_w, conf_conv_b, conf_ln_g, conf_ln_b, w_out, xa_norm_g, mem_norm_g, w_q, w_kv, w_o, ffn_norm_g, w_up, ffn_conv_w, ffn_conv_b, w_down, final_norm_g):
    bsz, seq, d = x.shape
    assert d == D_MODEL and seq % TM == 0 and mem.shape == (bsz, N_MEM, D_MODEL)
    assert w_in.shape[0] == 1
    tiles_per_seq = seq // TM
    n_tiles = bsz * tiles_per_seq
    row = lambda v: v.reshape(1, -1)

    k, v = pl.pallas_call(
        _kv_kernel,
        grid=(bsz,),
        in_specs=[pl.BlockSpec((None, N_MEM, D_MODEL), lambda b: (b, 0, 0)),
                  pl.BlockSpec((1, D_MODEL), lambda b: (0, 0)),
                  pl.BlockSpec((None, D_MODEL, 2 * D_MODEL), lambda b: (0, 0, 0), pipeline_mode=pl.Buffered(1))],
        out_specs=[pl.BlockSpec((None, N_MEM, D_MODEL), lambda b: (b, 0, 0))] * 2,
        out_shape=[jax.ShapeDtypeStruct((bsz, N_MEM, D_MODEL), _BF16)] * 2,
        compiler_params=pltpu.CompilerParams(dimension_semantics=("arbitrary",),
                                             vmem_limit_bytes=VMEM_LIMIT_KV),
        name="kv_proj",
    )(mem, row(mem_norm_g[0]), w_kv)

    def mix_rows(s):
        j = jnp.minimum(s, n_tiles - 1)
        return (j // tiles_per_seq, j % tiles_per_seq, 0)

    def out_rows(s):
        j = jnp.maximum(s - 1, 0)
        return (j // tiles_per_seq, j % tiles_per_seq, 0)

    kv_spec = pl.BlockSpec((None, N_MEM, D_MODEL), lambda s: (jnp.maximum(s - 1, 0) // tiles_per_seq, 0, 0),
                           pipeline_mode=pl.Buffered(1))
    f32_scratch = lambda rows, cols: pltpu.VMEM((rows, cols), _F32)
    bf16_scratch = lambda rows, cols: pltpu.VMEM((rows, cols), _BF16)
    hbm = pl.BlockSpec(memory_space=pl.ANY)
    return pl.pallas_call(
        functools.partial(_layer_kernel, tiles_per_seq),
        grid=(n_tiles + 1,),
        in_specs=[pl.BlockSpec((None, TM, D_MODEL), mix_rows), kv_spec, kv_spec,
                  _const_spec((1, D_MODEL)),
                  hbm,
                  _const_spec((LRU_CONV, D_LRU)),
                  _const_spec((1, D_LRU)),
                  _const_spec((D_LRU // HALF, HALF, 2 * HALF)),
                  _const_spec((1, D_LRU)),
                  _const_spec((1, D_LRU)),
                  _const_spec((1, D_LRU)),
                  _const_spec((CONF_CONV, D_CONF)),
                  _const_spec((1, D_CONF)),
                  _const_spec((1, D_CONF)),
                  _const_spec((1, D_CONF)),
                  hbm,
                  _const_spec((1, D_MODEL)),
                  hbm, hbm,
                  _const_spec((1, D_MODEL)),
                  hbm,
                  _const_spec((FFN_CONV, D_FF)),
                  _const_spec((1, D_FF)),
                  hbm,
                  _const_spec((1, D_MODEL))],
        out_specs=pl.BlockSpec((None, TM, D_MODEL), out_rows),
        out_shape=jax.ShapeDtypeStruct(x.shape, x.dtype),
        scratch_shapes=[bf16_scratch(D_MODEL, 2 * (D_LRU + D_CONF)),
                        bf16_scratch(D_LRU + D_CONF, D_MODEL),
                        bf16_scratch(D_MODEL, D_MODEL),
                        bf16_scratch(D_MODEL, D_MODEL),
                        bf16_scratch(D_MODEL, 2 * D_FF),
                        bf16_scratch(D_FF, D_MODEL),
                        pltpu.SemaphoreType.DMA((2,)),
                        f32_scratch(TM, D_MODEL),
                        f32_scratch(TM, D_MODEL),
                        f32_scratch(SUBLANES + TM, D_LRU),
                        f32_scratch(CONF_HALO + TM, D_CONF),
                        f32_scratch(TM, D_CONF),
                        f32_scratch(TM, D_LRU),
                        f32_scratch(SUBLANES, D_LRU),
                        f32_scratch(MIX_ROWS, D_LRU),
                        f32_scratch(MIX_ROWS, D_LRU),
                        bf16_scratch(TM, D_LRU + D_CONF),
                        bf16_scratch(TM, D_MODEL),
                        f32_scratch(SUBLANES, D_FF),
                        pltpu.VMEM((2, SUBLANES + TM, FF_CHUNK), _F32),
                        f32_scratch(TM, FF_CHUNK),
                        bf16_scratch(TM, D_FF)],
        compiler_params=pltpu.CompilerParams(dimension_semantics=("arbitrary",),
                                             vmem_limit_bytes=VMEM_LIMIT_LAYER),
        name="layer",
    )(x, k, v,
      row(mix_norm_g[0]), w_in, lru_conv_w[0], row(lru_conv_b[0]),
      _block_diag_gates(lru_w_a[0], lru_w_x[0]).astype(_BF16), row(lru_b_a[0]), row(lru_b_x[0]),
      row(lru_lambda[0]), conf_conv_w[0], row(conf_conv_b[0]), row(conf_ln_g[0]), row(conf_ln_b[0]), w_out,
      row(xa_norm_g[0]), w_q, w_o,
      row(ffn_norm_g[0]), w_up, ffn_conv_w[0], row(ffn_conv_b[0]), w_down, row(final_norm_g))
```

```python
import functools
import math

import jax
import jax.numpy as jnp
from jax import lax
from jax.experimental import pallas as pl
from jax.experimental.pallas import tpu as pltpu

D_MODEL = 1024
N_MEM = 256
D_LRU = 512
D_CONF = 512
LRU_HD = 64
LRU_CONV = 4
RG_C = 8.0
CONF_CONV = 31
XA_HEADS = 4
XA_HD = 256
D_FF = 3072
FFN_CONV = 3
EPS = 1e-6

SUBLANES = 8
HALF = 256
TM = 512
MIX_ROWS = TM
CONV_COLS = 256
FF_CHUNK = 512
CONF_HALO = 32
VMEM_LIMIT_KV = 32 * 1024 * 1024
VMEM_LIMIT_LAYER = 63 * 1024 * 1024

_BF16 = jnp.bfloat16
_F32 = jnp.float32
_DONE = object()


def _rms(x, g):
    return x * lax.rsqrt(jnp.mean(x * x, axis=-1, keepdims=True) + EPS) * g


_LOG2E = math.log2(math.e)


def _gelu_tanh_times(x, y):
    c = math.sqrt(2.0 / math.pi)
    k1 = -2.0 * c * _LOG2E
    k3 = k1 * 0.044715
    e = jnp.exp2(x * (k1 + k3 * (x * x)))
    return (x * y) / (1.0 + e)


def _sigmoid(x):
    return 1.0 / (1.0 + jnp.exp2(x * (-_LOG2E)))


def _dot(a, b):
    return jnp.dot(a, b, preferred_element_type=_F32)


def _shift_rows_1(x):
    return pltpu.roll(x, 1, 0)


def _kv_kernel(mem_ref, g_ref, wkv_ref, k_ref, v_ref):
    m = _rms(mem_ref[...], g_ref[...]).astype(_BF16)
    kv = _dot(m, wkv_ref[...].astype(_BF16))
    k_ref[...] = (kv[:, :D_MODEL] * (XA_HD ** -0.5)).astype(_BF16)
    v_ref[...] = kv[:, D_MODEL:].astype(_BF16)


def _mix_carry_in(first_tile, xbuf, cbuf, hcar):
    @pl.when(first_tile)
    def _():
        xbuf[0:SUBLANES, :] = jnp.zeros((SUBLANES, D_LRU), _F32)
        cbuf[0:CONF_HALO, :] = jnp.zeros((CONF_HALO, D_CONF), _F32)
        hcar[...] = jnp.zeros((SUBLANES, D_LRU), _F32)

    @pl.when(jnp.logical_not(first_tile))
    def _():
        xbuf[0:SUBLANES, :] = xbuf[TM:TM + SUBLANES, :]
        cbuf[0:CONF_HALO, :] = cbuf[TM:TM + CONF_HALO, :]


def _conv4(r0, cw4_ref, cb4_ref, xbuf):
    ext = xbuf[r0:r0 + MIX_ROWS + SUBLANES, :]
    acc = cw4_ref[0:1, :] * ext
    for k in range(1, LRU_CONV):
        acc = _shift_rows_1(acc) + cw4_ref[k:k + 1, :] * ext
    return acc[SUBLANES:, :] + cb4_ref[...]


def _conv31(r0, n_rows, cols, cw31_ref, cb31_ref, cbuf):
    ext_rows = n_rows + SUBLANES
    win = cbuf[r0:r0 + CONF_HALO + n_rows, cols]
    sources = (win, pltpu.roll(win, 4, 0))
    parts = [[None, None], [None, None]]
    for m in range(2):
        for r in range(2):
            for q in range(CONF_HALO // SUBLANES):
                for s in range(2):
                    d = 8 * q + 4 * s + 2 * m + r
                    if d >= CONF_CONV:
                        continue
                    k = CONF_CONV - 1 - d
                    start = CONF_HALO - SUBLANES - SUBLANES * q
                    term = cw31_ref[k:k + 1, cols] * sources[s][start:start + ext_rows, :]
                    parts[m][r] = term if parts[m][r] is None else parts[m][r] + term
    even = parts[0][0] + _shift_rows_1(parts[0][1])
    odd = parts[1][0] + _shift_rows_1(parts[1][1])
    acc = even + pltpu.roll(odd, 2, 0)
    return acc[SUBLANES:, :] + cb31_ref[:, cols]


def _recurrence(a, u, carry, abuf, ubuf):
    groups = MIX_ROWS // SUBLANES
    a3 = a.reshape(groups, SUBLANES, D_LRU)
    u3 = u.reshape(groups, SUBLANES, D_LRU)
    sub = lax.broadcasted_iota(jnp.int32, (groups, SUBLANES, D_LRU), 1)
    for k in (1, 2, 4):
        keep = sub >= k
        a_prev = jnp.where(keep, pltpu.roll(a3, k, 1), 1.0)
        u_prev = jnp.where(keep, pltpu.roll(u3, k, 1), 0.0)
        u3 = u3 + a3 * u_prev
        a3 = a3 * a_prev
    abuf[...] = a3.reshape(MIX_ROWS, D_LRU)
    ubuf[...] = u3.reshape(MIX_ROWS, D_LRU)
    for gidx in range(groups):
        rows = pl.ds(gidx * SUBLANES, SUBLANES)
        hg = ubuf[rows, :] + abuf[rows, :] * carry
        ubuf[rows, :] = hg
        carry = jnp.broadcast_to(hg[SUBLANES - 1:SUBLANES, :], (SUBLANES, D_LRU))
    return ubuf[...], carry


def _mix_block(r0, state, x_ref, g_ref, win_ref, cw4_ref, cb4_ref, wgate_ref, ba_ref, bx_ref, lam_ref,
               cw31_ref, cb31_ref, lng_ref, lnb_ref, wout_ref, o_ref,
               xbuf, cbuf, cbbuf, gatebuf, hcar, abuf, ubuf, ycat):
    rows = slice(r0, r0 + MIX_ROWS)
    h = _rms(x_ref[rows, :], g_ref[...]).astype(_BF16)
    z = _dot(h, win_ref[:, 0:2 * D_LRU])
    xbuf[SUBLANES + r0:SUBLANES + r0 + MIX_ROWS, :] = z[:, 0:D_LRU]
    gatebuf[rows, :] = z[:, D_LRU:]
    yield

    xc = _conv4(r0, cw4_ref, cb4_ref, xbuf)
    xcb = xc.astype(_BF16)
    for j in range(D_LRU // HALF):
        gj = _dot(xcb[:, j * HALF:(j + 1) * HALF], wgate_ref[j])
        abuf[:, j * HALF:(j + 1) * HALF] = gj[:, :HALF]
        ubuf[:, j * HALF:(j + 1) * HALF] = gj[:, HALF:]
    r = _sigmoid(abuf[...] + ba_ref[...])
    i = _sigmoid(ubuf[...] + bx_ref[...])
    nlam = -lam_ref[...]
    softplus = jnp.maximum(nlam, 0.0) + jnp.log1p(jnp.exp(-jnp.abs(nlam)))
    log_a = (-RG_C * softplus) * r
    a = jnp.exp(log_a)
    u = jnp.sqrt(-jnp.tanh(log_a) * (a * a + 1.0)) * (i * xc)
    assert state["next_row"] == r0, "row blocks must run their second piece in order"
    carry = hcar[...] if r0 == 0 else state["carry"]
    hseq, carry = _recurrence(a, u, carry, abuf, ubuf)
    state.update(next_row=r0 + MIX_ROWS, carry=carry)
    if r0 + MIX_ROWS == TM:
        hcar[...] = carry
    ycat[rows, 0:D_LRU] = _gelu_tanh_times(gatebuf[rows, :], hseq).astype(_BF16)

    brows = slice(CONF_HALO + r0, CONF_HALO + r0 + MIX_ROWS)
    for c0 in range(0, D_CONF, CONV_COLS):
        yield
        cols = slice(c0, c0 + CONV_COLS)
        cbuf[brows, cols] = _dot(h, win_ref[:, 2 * D_LRU + c0:2 * D_LRU + c0 + CONV_COLS])
        cbbuf[rows, cols] = _dot(h, win_ref[:, 2 * D_LRU + D_CONF + c0:2 * D_LRU + D_CONF + c0 + CONV_COLS])
        cbuf[brows, cols] = cbuf[brows, cols] * _sigmoid(cbbuf[rows, cols])
        cbbuf[rows, cols] = _conv31(r0, MIX_ROWS, cols, cw31_ref, cb31_ref, cbuf)
    c = cbbuf[rows, :]
    mu = jnp.mean(c, axis=-1, keepdims=True)
    cc = c - mu
    var = jnp.mean(cc * cc, axis=-1, keepdims=True)
    cn = cc * lax.rsqrt(var + EPS) * lng_ref[...] + lnb_ref[...]
    ycat[rows, D_LRU:] = (cn * _sigmoid(cn)).astype(_BF16)
    yield
    o_ref[rows, :] = _dot(ycat[rows, :], wout_ref[...])
    o_ref[rows, :] = o_ref[rows, :] + x_ref[rows, :]


def _xattn(x_ref, g_ref, wq_ref, k_ref, v_ref, wo_ref, o_ref, obuf, sbuf):
    h = _rms(x_ref[...], g_ref[...]).astype(_BF16)
    o_ref[...] = _dot(h, wq_ref[...])
    q = o_ref[...].astype(_BF16)
    yield
    heads = [slice(hd * XA_HD, (hd + 1) * XA_HD) for hd in range(XA_HEADS)]
    for hd, cols in enumerate(heads):
        sbuf[hd // 2, 0:TM, heads[hd % 2]] = lax.dot_general(
            q[:, cols], k_ref[:, cols], (((1,), (1,)), ((), ())), preferred_element_type=_F32)
    yield
    for hd, cols in enumerate(heads):
        s = sbuf[hd // 2, 0:TM, heads[hd % 2]]
        e = jnp.exp(s - jnp.max(s, axis=-1, keepdims=True))
        p = e * (1.0 / jnp.sum(e, axis=-1, keepdims=True))
        obuf[:, cols] = _dot(p.astype(_BF16), v_ref[:, cols]).astype(_BF16)
    yield
    o_ref[...] = _dot(obuf[...], wo_ref[...])
    o_ref[...] = o_ref[...] + x_ref[...]


def _ffn(x_ref, g_ref, wup_ref, cw_ref, cb_ref, wdown_ref, gf_ref, o_ref, ghalo, gbuf, ubuf2, actbuf):
    h = _rms(x_ref[...], g_ref[...]).astype(_BF16)
    for idx, c0 in enumerate(range(0, D_FF, FF_CHUNK)):
        cols = slice(c0, c0 + FF_CHUNK)
        slot = idx % 2
        gbuf[slot, 0:SUBLANES, :] = ghalo[:, cols]
        gbuf[slot, SUBLANES:SUBLANES + TM, :] = _dot(h, wup_ref[:, cols])
        ubuf2[...] = _dot(h, wup_ref[:, D_FF + c0:D_FF + c0 + FF_CHUNK])
        ghalo[:, cols] = gbuf[slot, TM:TM + SUBLANES, :]
        ext = gbuf[slot]
        acc = cw_ref[0:1, cols] * ext
        for k in range(1, FFN_CONV):
            acc = _shift_rows_1(acc) + cw_ref[k:k + 1, cols] * ext
        gc = acc[SUBLANES:, :] + cb_ref[:, cols]
        actbuf[:, cols] = _gelu_tanh_times(gc, ubuf2[...]).astype(_BF16)
        yield
    o_ref[...] = _dot(actbuf[...], wdown_ref[...])
    o_ref[...] = _rms(x_ref[...] + o_ref[...], gf_ref[...])


def _load_weights_as_bf16(weights, stages, sems):
    chunks = [(src, dst, r, c)
              for src, dst in weights
              for r in range(0, dst.shape[0], TM)
              for c in range(0, dst.shape[1], D_MODEL)]

    def chunk_copy(i):
        src, _, r, c = chunks[i]
        return pltpu.make_async_copy(src.at[0, pl.ds(r, TM), pl.ds(c, D_MODEL)], stages[i % 2], sems.at[i % 2])

    chunk_copy(0).start()
    for i, (_, dst, r, c) in enumerate(chunks):
        if i + 1 < len(chunks):
            chunk_copy(i + 1).start()
        chunk_copy(i).wait()
        dst[r:r + TM, c:c + D_MODEL] = stages[i % 2][...].astype(_BF16)


def _layer_kernel(tiles_per_seq,
                  x_ref, k_ref, v_ref,
                  g_mix, win_hbm, cw4_ref, cb4_ref, wgate_ref, ba_ref, bx_ref, lam_ref,
                  cw31_ref, cb31_ref, lng_ref, lnb_ref, wout_hbm,
                  g_xa, wq_hbm, wo_hbm,
                  g_ffn, wup_hbm, cwf_ref, cbf_ref, wdown_hbm, g_final,
                  o_ref,
                  win_ref, wout_ref, wq_ref, wo_ref, wup_ref, wdown_ref, wsem,
                  x1buf, x2buf, xbuf, cbuf, cbbuf, gatebuf, hcar, abuf, ubuf, ycat, obuf, ghalo, gbuf, ubuf2, actbuf):
    s = pl.program_id(0)
    n_tiles = pl.num_programs(0) - 1
    mix_tile = jnp.minimum(s, n_tiles - 1)
    out_tile = jnp.maximum(s - 1, 0)

    @pl.when(s == 0)
    def _():
        _load_weights_as_bf16([(win_hbm, win_ref), (wq_hbm, wq_ref), (wo_hbm, wo_ref), (wup_hbm, wup_ref),
                               (wdown_hbm, wdown_ref), (wout_hbm, wout_ref)], (x1buf, x2buf), wsem)
        x1buf[...] = jnp.zeros((TM, D_MODEL), _F32)

    @pl.when(out_tile % tiles_per_seq == 0)
    def _():
        ghalo[...] = jnp.zeros((SUBLANES, D_FF), _F32)

    _mix_carry_in(mix_tile % tiles_per_seq == 0, xbuf, cbuf, hcar)

    state = {"next_row": 0, "carry": None}
    m0 = _mix_block(0, state, x_ref, g_mix, win_ref, cw4_ref, cb4_ref, wgate_ref, ba_ref, bx_ref, lam_ref,
                    cw31_ref, cb31_ref, lng_ref, lnb_ref, wout_ref, x1buf,
                    xbuf, cbuf, cbbuf, gatebuf, hcar, abuf, ubuf, ycat)
    xa = _xattn(x1buf, g_xa, wq_ref, k_ref, v_ref, wo_ref, x2buf, obuf, gbuf)
    ff = _ffn(x2buf, g_ffn, wup_ref, cwf_ref, cbf_ref, wdown_ref, g_final, o_ref, ghalo, gbuf, ubuf2, actbuf)
    schedule = [m0, xa,
                xa, m0, m0, xa,
                xa,
                ff, ff, m0,
                ff, ff, ff, ff,
                ff, m0]
    live = {id(g): True for g in (m0, xa, ff)}
    for g in schedule:
        live[id(g)] = next(g, _DONE) is not _DONE
    assert not any(live.values()), "schedule must exhaust every stage"


def _const_spec(shape):
    nd = len(shape)
    return pl.BlockSpec(shape, lambda s: (0,) * nd, pipeline_mode=pl.Buffered(1))


def _block_diag_gates(w_a, w_x):
    heads_per_half = HALF // LRU_HD
    eye = jnp.eye(heads_per_half, dtype=w_a.dtype)

    def bd(w):
        w = w.reshape(D_LRU // HALF, heads_per_half, LRU_HD, LRU_HD)
        return jnp.einsum("jhik,hg->jhigk", w, eye).reshape(D_LRU // HALF, HALF, HALF)

    return jnp.concatenate([bd(w_a), bd(w_x)], axis=-1)


def kernel(x, mem, mix_norm_g, w_in, lru_conv_w, lru_conv_b, lru_w_a, lru_b_a, lru_w_x, lru_b_x, lru_lambda, conf_conv_w, conf_conv_b, conf_ln_g, conf_ln_b, w_out, xa_norm_g, mem_norm_g, w_q, w_kv, w_o, ffn_norm_g, w_up, ffn_conv_w, ffn_conv_b, w_down, final_norm_g):
    bsz, seq, d = x.shape
    assert d == D_MODEL and seq % TM == 0 and mem.shape == (bsz, N_MEM, D_MODEL)
    assert w_in.shape[0] == 1
    tiles_per_seq = seq // TM
    n_tiles = bsz * tiles_per_seq
    row = lambda v: v.reshape(1, -1)

    k, v = pl.pallas_call(
        _kv_kernel,
        grid=(bsz,),
        in_specs=[pl.BlockSpec((None, N_MEM, D_MODEL), lambda b: (b, 0, 0)),
                  pl.BlockSpec((1, D_MODEL), lambda b: (0, 0)),
                  pl.BlockSpec((None, D_MODEL, 2 * D_MODEL), lambda b: (0, 0, 0), pipeline_mode=pl.Buffered(1))],
        out_specs=[pl.BlockSpec((None, N_MEM, D_MODEL), lambda b: (b, 0, 0))] * 2,
        out_shape=[jax.ShapeDtypeStruct((bsz, N_MEM, D_MODEL), _BF16)] * 2,
        compiler_params=pltpu.CompilerParams(dimension_semantics=("arbitrary",),
                                             vmem_limit_bytes=VMEM_LIMIT_KV),
        name="kv_proj",
    )(mem, row(mem_norm_g[0]), w_kv)

    def mix_rows(s):
        j = jnp.minimum(s, n_tiles - 1)
        return (j // tiles_per_seq, j % tiles_per_seq, 0)

    def out_rows(s):
        j = jnp.maximum(s - 1, 0)
        return (j // tiles_per_seq, j % tiles_per_seq, 0)

    kv_spec = pl.BlockSpec((None, N_MEM, D_MODEL), lambda s: (jnp.maximum(s - 1, 0) // tiles_per_seq, 0, 0),
                           pipeline_mode=pl.Buffered(1))
    f32_scratch = lambda rows, cols: pltpu.VMEM((rows, cols), _F32)
    bf16_scratch = lambda rows, cols: pltpu.VMEM((rows, cols), _BF16)
    hbm = pl.BlockSpec(memory_space=pl.ANY)
    return pl.pallas_call(
        functools.partial(_layer_kernel, tiles_per_seq),
        grid=(n_tiles + 1,),
        in_specs=[pl.BlockSpec((None, TM, D_MODEL), mix_rows), kv_spec, kv_spec,
                  _const_spec((1, D_MODEL)),
                  hbm,
                  _const_spec((LRU_CONV, D_LRU)),
                  _const_spec((1, D_LRU)),
                  _const_spec((D_LRU // HALF, HALF, 2 * HALF)),
                  _const_spec((1, D_LRU)),
                  _const_spec((1, D_LRU)),
                  _const_spec((1, D_LRU)),
                  _const_spec((CONF_CONV, D_CONF)),
                  _const_spec((1, D_CONF)),
                  _const_spec((1, D_CONF)),
                  _const_spec((1, D_CONF)),
                  hbm,
                  _const_spec((1, D_MODEL)),
                  hbm, hbm,
                  _const_spec((1, D_MODEL)),
                  hbm,
                  _const_spec((FFN_CONV, D_FF)),
                  _const_spec((1, D_FF)),
                  hbm,
                  _const_spec((1, D_MODEL))],
        out_specs=pl.BlockSpec((None, TM, D_MODEL), out_rows),
        out_shape=jax.ShapeDtypeStruct(x.shape, x.dtype),
        scratch_shapes=[bf16_scratch(D_MODEL, 2 * (D_LRU + D_CONF)),
                        bf16_scratch(D_LRU + D_CONF, D_MODEL),
                        bf16_scratch(D_MODEL, D_MODEL),
                        bf16_scratch(D_MODEL, D_MODEL),
                        bf16_scratch(D_MODEL, 2 * D_FF),
                        bf16_scratch(D_FF, D_MODEL),
                        pltpu.SemaphoreType.DMA((2,)),
                        f32_scratch(TM, D_MODEL),
                        f32_scratch(TM, D_MODEL),
                        f32_scratch(SUBLANES + TM, D_LRU),
                        f32_scratch(CONF_HALO + TM, D_CONF),
                        f32_scratch(TM, D_CONF),
                        f32_scratch(TM, D_LRU),
                        f32_scratch(SUBLANES, D_LRU),
                        f32_scratch(MIX_ROWS, D_LRU),
                        f32_scratch(MIX_ROWS, D_LRU),
                        bf16_scratch(TM, D_LRU + D_CONF),
                        bf16_scratch(TM, D_MODEL),
                        f32_scratch(SUBLANES, D_FF),
                        pltpu.VMEM((2, SUBLANES + TM, FF_CHUNK), _F32),
                        f32_scratch(TM, FF_CHUNK),
                        bf16_scratch(TM, D_FF)],
        compiler_params=pltpu.CompilerParams(dimension_semantics=("arbitrary",),
                                             vmem_limit_bytes=VMEM_LIMIT_LAYER),
        name="layer",
    )(x, k, v,
      row(mix_norm_g[0]), w_in, lru_conv_w[0], row(lru_conv_b[0]),
      _block_diag_gates(lru_w_a[0], lru_w_x[0]).astype(_BF16), row(lru_b_a[0]), row(lru_b_x[0]),
      row(lru_lambda[0]), conf_conv_w[0], row(conf_conv_b[0]), row(conf_ln_g[0]), row(conf_ln_b[0]), w_out,
      row(xa_norm_g[0]), w_q, w_o,
      row(ffn_norm_g[0]), w_up, ffn_conv_w[0], row(ffn_conv_b[0]), w_down, row(final_norm_g))
```

```python
import functools
import math

import jax
import jax.numpy as jnp
from jax import lax
from jax.experimental import pallas as pl
from jax.experimental.pallas import tpu as pltpu

D_MODEL = 1024
N_MEM = 256
D_LRU = 512
D_CONF = 512
LRU_HD = 64
LRU_CONV = 4
RG_C = 8.0
CONF_CONV = 31
XA_HEADS = 4
XA_HD = 256
D_FF = 3072
FFN_CONV = 3
EPS = 1e-6

SUBLANES = 8
HALF = 256
TM = 512
MIX_ROWS = TM
CONV_COLS = 256
FF_CHUNK = 512
CONF_HALO = 32
VMEM_LIMIT_KV = 32 * 1024 * 1024
VMEM_LIMIT_LAYER = 63 * 1024 * 1024

_BF16 = jnp.bfloat16
_F32 = jnp.float32
_DONE = object()


def _rms(x, g):
    return x * lax.rsqrt(jnp.mean(x * x, axis=-1, keepdims=True) + EPS) * g


_LOG2E = math.log2(math.e)


def _gelu_tanh_times(x, y):
    c = math.sqrt(2.0 / math.pi)
    k1 = -2.0 * c * _LOG2E
    k3 = k1 * 0.044715
    e = jnp.exp2(x * (k1 + k3 * (x * x)))
    return (x * y) / (1.0 + e)


def _sigmoid(x):
    return 1.0 / (1.0 + jnp.exp2(x * (-_LOG2E)))


def _dot(a, b):
    return jnp.dot(a, b, preferred_element_type=_F32)


def _shift_rows_1(x):
    return pltpu.roll(x, 1, 0)


def _kv_kernel(mem_ref, g_ref, wkv_ref, k_ref, v_ref):
    m = _rms(mem_ref[...], g_ref[...]).astype(_BF16)
    kv = _dot(m, wkv_ref[...].astype(_BF16))
    k_ref[...] = (kv[:, :D_MODEL] * (XA_HD ** -0.5)).astype(_BF16)
    v_ref[...] = kv[:, D_MODEL:].astype(_BF16)


def _mix_carry_in(first_tile, xbuf, cbuf, hcar):
    @pl.when(first_tile)
    def _():
        xbuf[0:SUBLANES, :] = jnp.zeros((SUBLANES, D_LRU), _F32)
        cbuf[0:CONF_HALO, :] = jnp.zeros((CONF_HALO, D_CONF), _F32)
        hcar[...] = jnp.zeros((SUBLANES, D_LRU), _F32)

    @pl.when(jnp.logical_not(first_tile))
    def _():
        xbuf[0:SUBLANES, :] = xbuf[TM:TM + SUBLANES, :]
        cbuf[0:CONF_HALO, :] = cbuf[TM:TM + CONF_HALO, :]


def _conv4(r0, cw4_ref, cb4_ref, xbuf):
    ext = xbuf[r0:r0 + MIX_ROWS + SUBLANES, :]
    acc = cw4_ref[0:1, :] * ext
    for k in range(1, LRU_CONV):
        acc = _shift_rows_1(acc) + cw4_ref[k:k + 1, :] * ext
    return acc[SUBLANES:, :] + cb4_ref[...]


def _conv31(r0, n_rows, cols, cw31_ref, cb31_ref, cbuf):
    ext_rows = n_rows + SUBLANES
    win = cbuf[r0:r0 + CONF_HALO + n_rows, cols]
    sources = (win, pltpu.roll(win, 4, 0))
    parts = [[None, None], [None, None]]
    for m in range(2):
        for r in range(2):
            for q in range(CONF_HALO // SUBLANES):
                for s in range(2):
                    d = 8 * q + 4 * s + 2 * m + r
                    if d >= CONF_CONV:
                        continue
                    k = CONF_CONV - 1 - d
                    start = CONF_HALO - SUBLANES - SUBLANES * q
                    term = cw31_ref[k:k + 1, cols] * sources[s][start:start + ext_rows, :]
                    parts[m][r] = term if parts[m][r] is None else parts[m][r] + term
    even = parts[0][0] + _shift_rows_1(parts[0][1])
    odd = parts[1][0] + _shift_rows_1(parts[1][1])
    acc = even + pltpu.roll(odd, 2, 0)
    return acc[SUBLANES:, :] + cb31_ref[:, cols]


def _recurrence(a, u, carry, abuf, ubuf):
    groups = MIX_ROWS // SUBLANES
    a3 = a.reshape(groups, SUBLANES, D_LRU)
    u3 = u.reshape(groups, SUBLANES, D_LRU)
    sub = lax.broadcasted_iota(jnp.int32, (groups, SUBLANES, D_LRU), 1)
    for k in (1, 2, 4):
        keep = sub >= k
        a_prev = jnp.where(keep, pltpu.roll(a3, k, 1), 1.0)
        u_prev = jnp.where(keep, pltpu.roll(u3, k, 1), 0.0)
        u3 = u3 + a3 * u_prev
        a3 = a3 * a_prev
    abuf[...] = a3.reshape(MIX_ROWS, D_LRU)
    ubuf[...] = u3.reshape(MIX_ROWS, D_LRU)
    for gidx in range(groups):
        rows = pl.ds(gidx * SUBLANES, SUBLANES)
        hg = ubuf[rows, :] + abuf[rows, :] * carry
        ubuf[rows, :] = hg
        carry = jnp.broadcast_to(hg[SUBLANES - 1:SUBLANES, :], (SUBLANES, D_LRU))
    return ubuf[...], carry


def _mix_block(r0, state, x_ref, g_ref, win_ref, cw4_ref, cb4_ref, wgate_ref, ba_ref, bx_ref, lam_ref,
               cw31_ref, cb31_ref, lng_ref, lnb_ref, wout_ref, o_ref,
               xbuf, cbuf, cbbuf, gatebuf, hcar, abuf, ubuf, ycat):
    rows = slice(r0, r0 + MIX_ROWS)
    h = _rms(x_ref[rows, :], g_ref[...]).astype(_BF16)
    z = _dot(h, win_ref[:, 0:2 * D_LRU])
    xbuf[SUBLANES + r0:SUBLANES + r0 + MIX_ROWS, :] = z[:, 0:D_LRU]
    gatebuf[rows, :] = z[:, D_LRU:]
    yield

    xc = _conv4(r0, cw4_ref, cb4_ref, xbuf)
    xcb = xc.astype(_BF16)
    for j in range(D_LRU // HALF):
        gj = _dot(xcb[:, j * HALF:(j + 1) * HALF], wgate_ref[j])
        abuf[:, j * HALF:(j + 1) * HALF] = gj[:, :HALF]
        ubuf[:, j * HALF:(j + 1) * HALF] = gj[:, HALF:]
    r = _sigmoid(abuf[...] + ba_ref[...])
    i = _sigmoid(ubuf[...] + bx_ref[...])
    nlam = -lam_ref[...]
    softplus = jnp.maximum(nlam, 0.0) + jnp.log1p(jnp.exp(-jnp.abs(nlam)))
    log_a = (-RG_C * softplus) * r
    a = jnp.exp(log_a)
    u = jnp.sqrt(-jnp.tanh(log_a) * (a * a + 1.0)) * (i * xc)
    assert state["next_row"] == r0, "row blocks must run their second piece in order"
    carry = hcar[...] if r0 == 0 else state["carry"]
    hseq, carry = _recurrence(a, u, carry, abuf, ubuf)
    state.update(next_row=r0 + MIX_ROWS, carry=carry)
    if r0 + MIX_ROWS == TM:
        hcar[...] = carry
    ycat[rows, 0:D_LRU] = _gelu_tanh_times(gatebuf[rows, :], hseq).astype(_BF16)

    brows = slice(CONF_HALO + r0, CONF_HALO + r0 + MIX_ROWS)
    for c0 in range(0, D_CONF, CONV_COLS):
        yield
        cols = slice(c0, c0 + CONV_COLS)
        cbuf[brows, cols] = _dot(h, win_ref[:, 2 * D_LRU + c0:2 * D_LRU + c0 + CONV_COLS])
        cbbuf[rows, cols] = _dot(h, win_ref[:, 2 * D_LRU + D_CONF + c0:2 * D_LRU + D_CONF + c0 + CONV_COLS])
        cbuf[brows, cols] = cbuf[brows, cols] * _sigmoid(cbbuf[rows, cols])
        cbbuf[rows, cols] = _conv31(r0, MIX_ROWS, cols, cw31_ref, cb31_ref, cbuf)
    c = cbbuf[rows, :]
    mu = jnp.mean(c, axis=-1, keepdims=True)
    cc = c - mu
    var = jnp.mean(cc * cc, axis=-1, keepdims=True)
    cn = cc * lax.rsqrt(var + EPS) * lng_ref[...] + lnb_ref[...]
    ycat[rows, D_LRU:] = (cn * _sigmoid(cn)).astype(_BF16)
    yield
    o_ref[rows, :] = _dot(ycat[rows, :], wout_ref[...])
    o_ref[rows, :] = o_ref[rows, :] + x_ref[rows, :]


def _xattn(x_ref, g_ref, wq_ref, k_ref, v_ref, wo_ref, o_ref, obuf, sbuf):
    h = _rms(x_ref[...], g_ref[...]).astype(_BF16)
    o_ref[...] = _dot(h, wq_ref[...])
    q = o_ref[...].astype(_BF16)
    yield
    heads = [slice(hd * XA_HD, (hd + 1) * XA_HD) for hd in range(XA_HEADS)]
    for hd, cols in enumerate(heads):
        sbuf[hd // 2, 0:TM, heads[hd % 2]] = lax.dot_general(
            q[:, cols], k_ref[:, cols], (((1,), (1,)), ((), ())), preferred_element_type=_F32)
    yield
    for hd, cols in enumerate(heads):
        s = sbuf[hd // 2, 0:TM, heads[hd % 2]]
        e = jnp.exp(s - jnp.max(s, axis=-1, keepdims=True))
        p = e * (1.0 / jnp.sum(e, axis=-1, keepdims=True))
        obuf[:, cols] = _dot(p.astype(_BF16), v_ref[:, cols]).astype(_BF16)
    yield
    o_ref[...] = _dot(obuf[...], wo_ref[...])
    o_ref[...] = o_ref[...] + x_ref[...]


def _ffn(x_ref, g_ref, wup_ref, cw_ref, cb_ref, wdown_ref, gf_ref, o_ref, ghalo, gbuf, ubuf2, actbuf):
    h = _rms(x_ref[...], g_ref[...]).astype(_BF16)
    for idx, c0 in enumerate(range(0, D_FF, FF_CHUNK)):
        cols = slice(c0, c0 + FF_CHUNK)
        slot = idx % 2
        gbuf[slot, 0:SUBLANES, :] = ghalo[:, cols]
        gbuf[slot, SUBLANES:SUBLANES + TM, :] = _dot(h, wup_ref[:, cols])
        ubuf2[...] = _dot(h, wup_ref[:, D_FF + c0:D_FF + c0 + FF_CHUNK])
        ghalo[:, cols] = gbuf[slot, TM:TM + SUBLANES, :]
        ext = gbuf[slot]
        acc = cw_ref[0:1, cols] * ext
        for k in range(1, FFN_CONV):
            acc = _shift_rows_1(acc) + cw_ref[k:k + 1, cols] * ext
        gc = acc[SUBLANES:, :] + cb_ref[:, cols]
        actbuf[:, cols] = _gelu_tanh_times(gc, ubuf2[...]).astype(_BF16)
        yield
    o_ref[...] = _dot(actbuf[...], wdown_ref[...])
    o_ref[...] = _rms(x_ref[...] + o_ref[...], gf_ref[...])


def _load_weights_as_bf16(weights, stages, sems):
    chunks = [(src, dst, r, c)
              for src, dst in weights
              for r in range(0, dst.shape[0], TM)
              for c in range(0, dst.shape[1], D_MODEL)]

    def chunk_copy(i):
        src, _, r, c = chunks[i]
        return pltpu.make_async_copy(src.at[0, pl.ds(r, TM), pl.ds(c, D_MODEL)], stages[i % 2], sems.at[i % 2])

    chunk_copy(0).start()
    for i, (_, dst, r, c) in enumerate(chunks):
        if i + 1 < len(chunks):
            chunk_copy(i + 1).start()
        chunk_copy(i).wait()
        dst[r:r + TM, c:c + D_MODEL] = stages[i % 2][...].astype(_BF16)


def _layer_kernel(tiles_per_seq,
                  x_ref, k_ref, v_ref,
                  g_mix, win_hbm, cw4_ref, cb4_ref, wgate_ref, ba_ref, bx_ref, lam_ref,
                  cw31_ref, cb31_ref, lng_ref, lnb_ref, wout_hbm,
                  g_xa, wq_hbm, wo_hbm,
                  g_ffn, wup_hbm, cwf_ref, cbf_ref, wdown_hbm, g_final,
                  o_ref,
                  win_ref, wout_ref, wq_ref, wo_ref, wup_ref, wdown_ref, wsem,
                  x1buf, x2buf, xbuf, cbuf, cbbuf, gatebuf, hcar, abuf, ubuf, ycat, obuf, ghalo, gbuf, ubuf2, actbuf):
    s = pl.program_id(0)
    n_tiles = pl.num_programs(0) - 1
    mix_tile = jnp.minimum(s, n_tiles - 1)
    out_tile = jnp.maximum(s - 1, 0)

    @pl.when(s == 0)
    def _():
        _load_weights_as_bf16([(win_hbm, win_ref), (wq_hbm, wq_ref), (wo_hbm, wo_ref), (wup_hbm, wup_ref),
                               (wdown_hbm, wdown_ref), (wout_hbm, wout_ref)], (x1buf, x2buf), wsem)
        x1buf[...] = jnp.zeros((TM, D_MODEL), _F32)

    @pl.when(out_tile % tiles_per_seq == 0)
    def _():
        ghalo[...] = jnp.zeros((SUBLANES, D_FF), _F32)

    _mix_carry_in(mix_tile % tiles_per_seq == 0, xbuf, cbuf, hcar)

    state = {"next_row": 0, "carry": None}
    m0 = _mix_block(0, state, x_ref, g_mix, win_ref, cw4_ref, cb4_ref, wgate_ref, ba_ref, bx_ref, lam_ref,
                    cw31_ref, cb31_ref, lng_ref, lnb_ref, wout_ref, x1buf,
                    xbuf, cbuf, cbbuf, gatebuf, hcar, abuf, ubuf, ycat)
    xa = _xattn(x1buf, g_xa, wq_ref, k_ref, v_ref, wo_ref, x2buf, obuf, gbuf)
    ff = _ffn(x2buf, g_ffn, wup_ref, cwf_ref, cbf_ref, wdown_ref, g_final, o_ref, ghalo, gbuf, ubuf2, actbuf)
    schedule = [m0, xa,
                xa, m0, xa,
                xa, m0,
                ff, ff, ff, m0,
                ff, ff, ff,
                ff, m0]
    live = {id(g): True for g in (m0, xa, ff)}
    for g in schedule:
        live[id(g)] = next(g, _DONE) is not _DONE
    assert not any(live.values()), "schedule must exhaust every stage"


def _const_spec(shape):
    nd = len(shape)
    return pl.BlockSpec(shape, lambda s: (0,) * nd, pipeline_mode=pl.Buffered(1))


def _block_diag_gates(w_a, w_x):
    heads_per_half = HALF // LRU_HD
    eye = jnp.eye(heads_per_half, dtype=w_a.dtype)

    def bd(w):
        w = w.reshape(D_LRU // HALF, heads_per_half, LRU_HD, LRU_HD)
        return jnp.einsum("jhik,hg->jhigk", w, eye).reshape(D_LRU // HALF, HALF, HALF)

    return jnp.concatenate([bd(w_a), bd(w_x)], axis=-1)


def kernel(x, mem, mix_norm_g, w_in, lru_conv_w, lru_conv_b, lru_w_a, lru_b_a, lru_w_x, lru_b_x, lru_lambda, conf_conv_w, conf_conv_b, conf_ln_g, conf_ln_b, w_out, xa_norm_g, mem_norm_g, w_q, w_kv, w_o, ffn_norm_g, w_up, ffn_conv_w, ffn_conv_b, w_down, final_norm_g):
    bsz, seq, d = x.shape
    assert d == D_MODEL and seq % TM == 0 and mem.shape == (bsz, N_MEM, D_MODEL)
    assert w_in.shape[0] == 1
    tiles_per_seq = seq // TM
    n_tiles = bsz * tiles_per_seq
    row = lambda v: v.reshape(1, -1)

    k, v = pl.pallas_call(
        _kv_kernel,
        grid=(bsz,),
        in_specs=[pl.BlockSpec((None, N_MEM, D_MODEL), lambda b: (b, 0, 0)),
                  pl.BlockSpec((1, D_MODEL), lambda b: (0, 0)),
                  pl.BlockSpec((None, D_MODEL, 2 * D_MODEL), lambda b: (0, 0, 0), pipeline_mode=pl.Buffered(1))],
        out_specs=[pl.BlockSpec((None, N_MEM, D_MODEL), lambda b: (b, 0, 0))] * 2,
        out_shape=[jax.ShapeDtypeStruct((bsz, N_MEM, D_MODEL), _BF16)] * 2,
        compiler_params=pltpu.CompilerParams(dimension_semantics=("arbitrary",),
                                             vmem_limit_bytes=VMEM_LIMIT_KV),
        name="kv_proj",
    )(mem, row(mem_norm_g[0]), w_kv)

    def mix_rows(s):
        j = jnp.minimum(s, n_tiles - 1)
        return (j // tiles_per_seq, j % tiles_per_seq, 0)

    def out_rows(s):
        j = jnp.maximum(s - 1, 0)
        return (j // tiles_per_seq, j % tiles_per_seq, 0)

    kv_spec = pl.BlockSpec((None, N_MEM, D_MODEL), lambda s: (jnp.maximum(s - 1, 0) // tiles_per_seq, 0, 0),
                           pipeline_mode=pl.Buffered(1))
    f32_scratch = lambda rows, cols: pltpu.VMEM((rows, cols), _F32)
    bf16_scratch = lambda rows, cols: pltpu.VMEM((rows, cols), _BF16)
    hbm = pl.BlockSpec(memory_space=pl.ANY)
    return pl.pallas_call(
        functools.partial(_layer_kernel, tiles_per_seq),
        grid=(n_tiles + 1,),
        in_specs=[pl.BlockSpec((None, TM, D_MODEL), mix_rows), kv_spec, kv_spec,
                  _const_spec((1, D_MODEL)),
                  hbm,
                  _const_spec((LRU_CONV, D_LRU)),
                  _const_spec((1, D_LRU)),
                  _const_spec((D_LRU // HALF, HALF, 2 * HALF)),
                  _const_spec((1, D_LRU)),
                  _const_spec((1, D_LRU)),
                  _const_spec((1, D_LRU)),
                  _const_spec((CONF_CONV, D_CONF)),
                  _const_spec((1, D_CONF)),
                  _const_spec((1, D_CONF)),
                  _const_spec((1, D_CONF)),
                  hbm,
                  _const_spec((1, D_MODEL)),
                  hbm, hbm,
                  _const_spec((1, D_MODEL)),
                  hbm,
                  _const_spec((FFN_CONV, D_FF)),
                  _const_spec((1, D_FF)),
                  hbm,
                  _const_spec((1, D_MODEL))],
        out_specs=pl.BlockSpec((None, TM, D_MODEL), out_rows),
        out_shape=jax.ShapeDtypeStruct(x.shape, x.dtype),
        scratch_shapes=[bf16_scratch(D_MODEL, 2 * (D_LRU + D_CONF)),
                        bf16_scratch(D_LRU + D_CONF, D_MODEL),
                        bf16_scratch(D_MODEL, D_MODEL),
                        bf16_scratch(D_MODEL, D_MODEL),
                        bf16_scratch(D_MODEL, 2 * D_FF),
                        bf16_scratch(D_FF, D_MODEL),
                        pltpu.SemaphoreType.DMA((2,)),
                        f32_scratch(TM, D_MODEL),
                        f32_scratch(TM, D_MODEL),
                        f32_scratch(SUBLANES + TM, D_LRU),
                        f32_scratch(CONF_HALO + TM, D_CONF),
                        f32_scratch(TM, D_CONF),
                        f32_scratch(TM, D_LRU),
                        f32_scratch(SUBLANES, D_LRU),
                        f32_scratch(MIX_ROWS, D_LRU),
                        f32_scratch(MIX_ROWS, D_LRU),
                        bf16_scratch(TM, D_LRU + D_CONF),
                        bf16_scratch(TM, D_MODEL),
                        f32_scratch(SUBLANES, D_FF),
                        pltpu.VMEM((2, SUBLANES + TM, FF_CHUNK), _F32),
                        f32_scratch(TM, FF_CHUNK),
                        bf16_scratch(TM, D_FF)],
        compiler_params=pltpu.CompilerParams(dimension_semantics=("arbitrary",),
                                             vmem_limit_bytes=VMEM_LIMIT_LAYER),
        name="layer",
    )(x, k, v,
      row(mix_norm_g[0]), w_in, lru_conv_w[0], row(lru_conv_b[0]),
      _block_diag_gates(lru_w_a[0], lru_w_x[0]).astype(_BF16), row(lru_b_a[0]), row(lru_b_x[0]),
      row(lru_lambda[0]), conf_conv_w[0], row(conf_conv_b[0]), row(conf_ln_g[0]), row(conf_ln_b[0]), w_out,
      row(xa_norm_g[0]), w_q, w_o,
      row(ffn_norm_g[0]), w_up, ffn_conv_w[0], row(ffn_conv_b[0]), w_down, row(final_norm_g))
```

```python
import functools
import math

import jax
import jax.numpy as jnp
from jax import lax
from jax.experimental import pallas as pl
from jax.experimental.pallas import tpu as pltpu

D_MODEL = 1024
N_MEM = 256
D_LRU = 512
D_CONF = 512
LRU_HD = 64
LRU_CONV = 4
RG_C = 8.0
CONF_CONV = 31
XA_HEADS = 4
XA_HD = 256
D_FF = 3072
FFN_CONV = 3
EPS = 1e-6

SUBLANES = 8
HALF = 256
TM = 512
MIX_ROWS = TM
CONV_COLS = 256
FF_CHUNK = 512
CONF_HALO = 32
VMEM_LIMIT_KV = 32 * 1024 * 1024
VMEM_LIMIT_LAYER = 63 * 1024 * 1024

_BF16 = jnp.bfloat16
_F32 = jnp.float32
_DONE = object()


def _rms(x, g):
    return x * lax.rsqrt(jnp.mean(x * x, axis=-1, keepdims=True) + EPS) * g


_LOG2E = math.log2(math.e)


def _gelu_tanh_times(x, y):
    c = math.sqrt(2.0 / math.pi)
    k1 = -2.0 * c * _LOG2E
    k3 = k1 * 0.044715
    e = jnp.exp2(x * (k1 + k3 * (x * x)))
    return (x * y) / (1.0 + e)


def _sigmoid(x):
    return 1.0 / (1.0 + jnp.exp2(x * (-_LOG2E)))


def _dot(a, b):
    return jnp.dot(a, b, preferred_element_type=_F32)


def _shift_rows_1(x):
    return pltpu.roll(x, 1, 0)


def _kv_kernel(mem_ref, g_ref, wkv_ref, k_ref, v_ref):
    m = _rms(mem_ref[...], g_ref[...]).astype(_BF16)
    kv = _dot(m, wkv_ref[...].astype(_BF16))
    k_ref[...] = (kv[:, :D_MODEL] * (XA_HD ** -0.5)).astype(_BF16)
    v_ref[...] = kv[:, D_MODEL:].astype(_BF16)


def _mix_carry_in(first_tile, xbuf, cbuf, hcar):
    @pl.when(first_tile)
    def _():
        xbuf[0:SUBLANES, :] = jnp.zeros((SUBLANES, D_LRU), _F32)
        cbuf[0:CONF_HALO, :] = jnp.zeros((CONF_HALO, D_CONF), _F32)
        hcar[...] = jnp.zeros((SUBLANES, D_LRU), _F32)

    @pl.when(jnp.logical_not(first_tile))
    def _():
        xbuf[0:SUBLANES, :] = xbuf[TM:TM + SUBLANES, :]
        cbuf[0:CONF_HALO, :] = cbuf[TM:TM + CONF_HALO, :]


def _conv4(r0, cw4_ref, cb4_ref, xbuf):
    ext = xbuf[r0:r0 + MIX_ROWS + SUBLANES, :]
    acc = cw4_ref[0:1, :] * ext
    for k in range(1, LRU_CONV):
        acc = _shift_rows_1(acc) + cw4_ref[k:k + 1, :] * ext
    return acc[SUBLANES:, :] + cb4_ref[...]


def _conv31(r0, n_rows, cols, cw31_ref, cb31_ref, cbuf):
    ext_rows = n_rows + SUBLANES
    win = cbuf[r0:r0 + CONF_HALO + n_rows, cols]
    sources = (win, pltpu.roll(win, 4, 0))
    parts = [[None, None], [None, None]]
    for m in range(2):
        for r in range(2):
            for q in range(CONF_HALO // SUBLANES):
                for s in range(2):
                    d = 8 * q + 4 * s + 2 * m + r
                    if d >= CONF_CONV:
                        continue
                    k = CONF_CONV - 1 - d
                    start = CONF_HALO - SUBLANES - SUBLANES * q
                    term = cw31_ref[k:k + 1, cols] * sources[s][start:start + ext_rows, :]
                    parts[m][r] = term if parts[m][r] is None else parts[m][r] + term
    even = parts[0][0] + _shift_rows_1(parts[0][1])
    odd = parts[1][0] + _shift_rows_1(parts[1][1])
    acc = even + pltpu.roll(odd, 2, 0)
    return acc[SUBLANES:, :] + cb31_ref[:, cols]


def _recurrence(a, u, carry, abuf, ubuf):
    groups = MIX_ROWS // SUBLANES
    a3 = a.reshape(groups, SUBLANES, D_LRU)
    u3 = u.reshape(groups, SUBLANES, D_LRU)
    sub = lax.broadcasted_iota(jnp.int32, (groups, SUBLANES, D_LRU), 1)
    for k in (1, 2, 4):
        keep = sub >= k
        a_prev = jnp.where(keep, pltpu.roll(a3, k, 1), 1.0)
        u_prev = jnp.where(keep, pltpu.roll(u3, k, 1), 0.0)
        u3 = u3 + a3 * u_prev
        a3 = a3 * a_prev
    abuf[...] = a3.reshape(MIX_ROWS, D_LRU)
    ubuf[...] = u3.reshape(MIX_ROWS, D_LRU)
    for gidx in range(groups):
        rows = pl.ds(gidx * SUBLANES, SUBLANES)
        hg = ubuf[rows, :] + abuf[rows, :] * carry
        ubuf[rows, :] = hg
        carry = jnp.broadcast_to(hg[SUBLANES - 1:SUBLANES, :], (SUBLANES, D_LRU))
    return ubuf[...], carry


def _mix_block(r0, state, x_ref, g_ref, win_ref, cw4_ref, cb4_ref, wgate_ref, ba_ref, bx_ref, lam_ref,
               cw31_ref, cb31_ref, lng_ref, lnb_ref, wout_ref, o_ref,
               xbuf, cbuf, cbbuf, gatebuf, hcar, abuf, ubuf, ycat):
    rows = slice(r0, r0 + MIX_ROWS)
    h = _rms(x_ref[rows, :], g_ref[...]).astype(_BF16)
    z = _dot(h, win_ref[:, 0:2 * D_LRU])
    xbuf[SUBLANES + r0:SUBLANES + r0 + MIX_ROWS, :] = z[:, 0:D_LRU]
    gatebuf[rows, :] = z[:, D_LRU:]
    yield

    xc = _conv4(r0, cw4_ref, cb4_ref, xbuf)
    xcb = xc.astype(_BF16)
    for j in range(D_LRU // HALF):
        gj = _dot(xcb[:, j * HALF:(j + 1) * HALF], wgate_ref[j])
        abuf[:, j * HALF:(j + 1) * HALF] = gj[:, :HALF]
        ubuf[:, j * HALF:(j + 1) * HALF] = gj[:, HALF:]
    r = _sigmoid(abuf[...] + ba_ref[...])
    i = _sigmoid(ubuf[...] + bx_ref[...])
    nlam = -lam_ref[...]
    softplus = jnp.maximum(nlam, 0.0) + jnp.log1p(jnp.exp(-jnp.abs(nlam)))
    log_a = (-RG_C * softplus) * r
    a = jnp.exp(log_a)
    u = jnp.sqrt(-jnp.tanh(log_a) * (a * a + 1.0)) * (i * xc)
    assert state["next_row"] == r0, "row blocks must run their second piece in order"
    carry = hcar[...] if r0 == 0 else state["carry"]
    hseq, carry = _recurrence(a, u, carry, abuf, ubuf)
    state.update(next_row=r0 + MIX_ROWS, carry=carry)
    if r0 + MIX_ROWS == TM:
        hcar[...] = carry
    ycat[rows, 0:D_LRU] = _gelu_tanh_times(gatebuf[rows, :], hseq).astype(_BF16)

    brows = slice(CONF_HALO + r0, CONF_HALO + r0 + MIX_ROWS)
    for c0 in range(0, D_CONF, CONV_COLS):
        yield
        cols = slice(c0, c0 + CONV_COLS)
        cbuf[brows, cols] = _dot(h, win_ref[:, 2 * D_LRU + c0:2 * D_LRU + c0 + CONV_COLS])
        cbbuf[rows, cols] = _dot(h, win_ref[:, 2 * D_LRU + D_CONF + c0:2 * D_LRU + D_CONF + c0 + CONV_COLS])
        cbuf[brows, cols] = cbuf[brows, cols] * _sigmoid(cbbuf[rows, cols])
        cbbuf[rows, cols] = _conv31(r0, MIX_ROWS, cols, cw31_ref, cb31_ref, cbuf)
    c = cbbuf[rows, :]
    mu = jnp.mean(c, axis=-1, keepdims=True)
    cc = c - mu
    var = jnp.mean(cc * cc, axis=-1, keepdims=True)
    cn = cc * lax.rsqrt(var + EPS) * lng_ref[...] + lnb_ref[...]
    ycat[rows, D_LRU:] = (cn * _sigmoid(cn)).astype(_BF16)
    yield
    o_ref[rows, :] = _dot(ycat[rows, :], wout_ref[...])
    o_ref[rows, :] = o_ref[rows, :] + x_ref[rows, :]


def _xattn(x_ref, g_ref, wq_ref, k_ref, v_ref, wo_ref, o_ref, obuf, sbuf):
    h = _rms(x_ref[...], g_ref[...]).astype(_BF16)
    o_ref[...] = _dot(h, wq_ref[...])
    q = o_ref[...].astype(_BF16)
    yield
    heads = [slice(hd * XA_HD, (hd + 1) * XA_HD) for hd in range(XA_HEADS)]
    for hd, cols in enumerate(heads):
        sbuf[hd // 2, 0:TM, heads[hd % 2]] = lax.dot_general(
            q[:, cols], k_ref[:, cols], (((1,), (1,)), ((), ())), preferred_element_type=_F32)
    yield
    for hd, cols in enumerate(heads):
        s = sbuf[hd // 2, 0:TM, heads[hd % 2]]
        e = jnp.exp(s - jnp.max(s, axis=-1, keepdims=True))
        p = e * (1.0 / jnp.sum(e, axis=-1, keepdims=True))
        obuf[:, cols] = _dot(p.astype(_BF16), v_ref[:, cols]).astype(_BF16)
    yield
    o_ref[...] = _dot(obuf[...], wo_ref[...])
    o_ref[...] = o_ref[...] + x_ref[...]


def _ffn(x_ref, g_ref, wup_ref, cw_ref, cb_ref, wdown_ref, gf_ref, o_ref, ghalo, gbuf, ubuf2, actbuf):
    h = _rms(x_ref[...], g_ref[...]).astype(_BF16)
    for idx, c0 in enumerate(range(0, D_FF, FF_CHUNK)):
        cols = slice(c0, c0 + FF_CHUNK)
        slot = idx % 2
        gbuf[slot, 0:SUBLANES, :] = ghalo[:, cols]
        gbuf[slot, SUBLANES:SUBLANES + TM, :] = _dot(h, wup_ref[:, cols])
        ubuf2[...] = _dot(h, wup_ref[:, D_FF + c0:D_FF + c0 + FF_CHUNK])
        ghalo[:, cols] = gbuf[slot, TM:TM + SUBLANES, :]
        ext = gbuf[slot]
        acc = cw_ref[0:1, cols] * ext
        for k in range(1, FFN_CONV):
            acc = _shift_rows_1(acc) + cw_ref[k:k + 1, cols] * ext
        gc = acc[SUBLANES:, :] + cb_ref[:, cols]
        actbuf[:, cols] = _gelu_tanh_times(gc, ubuf2[...]).astype(_BF16)
        yield
    o_ref[...] = _dot(actbuf[...], wdown_ref[...])
    o_ref[...] = _rms(x_ref[...] + o_ref[...], gf_ref[...])


def _load_weights_as_bf16(weights, stages, sems):
    chunks = [(src, dst, r, c)
              for src, dst in weights
              for r in range(0, dst.shape[0], TM)
              for c in range(0, dst.shape[1], D_MODEL)]

    def chunk_copy(i):
        src, _, r, c = chunks[i]
        return pltpu.make_async_copy(src.at[0, pl.ds(r, TM), pl.ds(c, D_MODEL)], stages[i % 2], sems.at[i % 2])

    chunk_copy(0).start()
    for i, (_, dst, r, c) in enumerate(chunks):
        if i + 1 < len(chunks):
            chunk_copy(i + 1).start()
        chunk_copy(i).wait()
        dst[r:r + TM, c:c + D_MODEL] = stages[i % 2][...].astype(_BF16)


def _layer_kernel(tiles_per_seq,
                  x_ref, k_ref, v_ref,
                  g_mix, win_hbm, cw4_ref, cb4_ref, wgate_ref, ba_ref, bx_ref, lam_ref,
                  cw31_ref, cb31_ref, lng_ref, lnb_ref, wout_hbm,
                  g_xa, wq_hbm, wo_hbm,
                  g_ffn, wup_hbm, cwf_ref, cbf_ref, wdown_hbm, g_final,
                  o_ref,
                  win_ref, wout_ref, wq_ref, wo_ref, wup_ref, wdown_ref, wsem,
                  x1buf, x2buf, xbuf, cbuf, cbbuf, gatebuf, hcar, abuf, ubuf, ycat, obuf, ghalo, gbuf, ubuf2, actbuf):
    s = pl.program_id(0)
    n_tiles = pl.num_programs(0) - 1
    mix_tile = jnp.minimum(s, n_tiles - 1)
    out_tile = jnp.maximum(s - 1, 0)

    @pl.when(s == 0)
    def _():
        _load_weights_as_bf16([(win_hbm, win_ref), (wq_hbm, wq_ref), (wo_hbm, wo_ref), (wup_hbm, wup_ref),
                               (wdown_hbm, wdown_ref), (wout_hbm, wout_ref)], (x1buf, x2buf), wsem)
        x1buf[...] = jnp.zeros((TM, D_MODEL), _F32)

    @pl.when(out_tile % tiles_per_seq == 0)
    def _():
        ghalo[...] = jnp.zeros((SUBLANES, D_FF), _F32)

    _mix_carry_in(mix_tile % tiles_per_seq == 0, xbuf, cbuf, hcar)

    state = {"next_row": 0, "carry": None}
    m0 = _mix_block(0, state, x_ref, g_mix, win_ref, cw4_ref, cb4_ref, wgate_ref, ba_ref, bx_ref, lam_ref,
                    cw31_ref, cb31_ref, lng_ref, lnb_ref, wout_ref, x1buf,
                    xbuf, cbuf, cbbuf, gatebuf, hcar, abuf, ubuf, ycat)
    xa = _xattn(x1buf, g_xa, wq_ref, k_ref, v_ref, wo_ref, x2buf, obuf, gbuf)
    ff = _ffn(x2buf, g_ffn, wup_ref, cwf_ref, cbf_ref, wdown_ref, g_final, o_ref, ghalo, gbuf, ubuf2, actbuf)
    schedule = [m0, xa,
                xa, m0, xa,
                xa, m0,
                ff, ff, ff, ff, ff, ff,
                m0,
                ff, m0]
    live = {id(g): True for g in (m0, xa, ff)}
    for g in schedule:
        live[id(g)] = next(g, _DONE) is not _DONE
    assert not any(live.values()), "schedule must exhaust every stage"


def _const_spec(shape):
    nd = len(shape)
    return pl.BlockSpec(shape, lambda s: (0,) * nd, pipeline_mode=pl.Buffered(1))


def _block_diag_gates(w_a, w_x):
    heads_per_half = HALF // LRU_HD
    eye = jnp.eye(heads_per_half, dtype=w_a.dtype)

    def bd(w):
        w = w.reshape(D_LRU // HALF, heads_per_half, LRU_HD, LRU_HD)
        return jnp.einsum("jhik,hg->jhigk", w, eye).reshape(D_LRU // HALF, HALF, HALF)

    return jnp.concatenate([bd(w_a), bd(w_x)], axis=-1)


def kernel(x, mem, mix_norm_g, w_in, lru_conv_w, lru_conv_b, lru_w_a, lru_b_a, lru_w_x, lru_b_x, lru_lambda, conf_conv_w, conf_conv_b, conf_ln_g, conf_ln_b, w_out, xa_norm_g, mem_norm_g, w_q, w_kv, w_o, ffn_norm_g, w_up, ffn_conv_w, ffn_conv_b, w_down, final_norm_g):
    bsz, seq, d = x.shape
    assert d == D_MODEL and seq % TM == 0 and mem.shape == (bsz, N_MEM, D_MODEL)
    assert w_in.shape[0] == 1
    tiles_per_seq = seq // TM
    n_tiles = bsz * tiles_per_seq
    row = lambda v: v.reshape(1, -1)

    k, v = pl.pallas_call(
        _kv_kernel,
        grid=(bsz,),
        in_specs=[pl.BlockSpec((None, N_MEM, D_MODEL), lambda b: (b, 0, 0)),
                  pl.BlockSpec((1, D_MODEL), lambda b: (0, 0)),
                  pl.BlockSpec((None, D_MODEL, 2 * D_MODEL), lambda b: (0, 0, 0), pipeline_mode=pl.Buffered(1))],
        out_specs=[pl.BlockSpec((None, N_MEM, D_MODEL), lambda b: (b, 0, 0))] * 2,
        out_shape=[jax.ShapeDtypeStruct((bsz, N_MEM, D_MODEL), _BF16)] * 2,
        compiler_params=pltpu.CompilerParams(dimension_semantics=("arbitrary",),
                                             vmem_limit_bytes=VMEM_LIMIT_KV),
        name="kv_proj",
    )(mem, row(mem_norm_g[0]), w_kv)

    def mix_rows(s):
        j = jnp.minimum(s, n_tiles - 1)
        return (j // tiles_per_seq, j % tiles_per_seq, 0)

    def out_rows(s):
        j = jnp.maximum(s - 1, 0)
        return (j // tiles_per_seq, j % tiles_per_seq, 0)

    kv_spec = pl.BlockSpec((None, N_MEM, D_MODEL), lambda s: (jnp.maximum(s - 1, 0) // tiles_per_seq, 0, 0),
                           pipeline_mode=pl.Buffered(1))
    f32_scratch = lambda rows, cols: pltpu.VMEM((rows, cols), _F32)
    bf16_scratch = lambda rows, cols: pltpu.VMEM((rows, cols), _BF16)
    hbm = pl.BlockSpec(memory_space=pl.ANY)
    return pl.pallas_call(
        functools.partial(_layer_kernel, tiles_per_seq),
        grid=(n_tiles + 1,),
        in_specs=[pl.BlockSpec((None, TM, D_MODEL), mix_rows), kv_spec, kv_spec,
                  _const_spec((1, D_MODEL)),
                  hbm,
                  _const_spec((LRU_CONV, D_LRU)),
                  _const_spec((1, D_LRU)),
                  _const_spec((D_LRU // HALF, HALF, 2 * HALF)),
                  _const_spec((1, D_LRU)),
                  _const_spec((1, D_LRU)),
                  _const_spec((1, D_LRU)),
                  _const_spec((CONF_CONV, D_CONF)),
                  _const_spec((1, D_CONF)),
                  _const_spec((1, D_CONF)),
                  _const_spec((1, D_CONF)),
                  hbm,
                  _const_spec((1, D_MODEL)),
                  hbm, hbm,
                  _const_spec((1, D_MODEL)),
                  hbm,
                  _const_spec((FFN_CONV, D_FF)),
                  _const_spec((1, D_FF)),
                  hbm,
                  _const_spec((1, D_MODEL))],
        out_specs=pl.BlockSpec((None, TM, D_MODEL), out_rows),
        out_shape=jax.ShapeDtypeStruct(x.shape, x.dtype),
        scratch_shapes=[bf16_scratch(D_MODEL, 2 * (D_LRU + D_CONF)),
                        bf16_scratch(D_LRU + D_CONF, D_MODEL),
                        bf16_scratch(D_MODEL, D_MODEL),
                        bf16_scratch(D_MODEL, D_MODEL),
                        bf16_scratch(D_MODEL, 2 * D_FF),
                        bf16_scratch(D_FF, D_MODEL),
                        pltpu.SemaphoreType.DMA((2,)),
                        f32_scratch(TM, D_MODEL),
                        f32_scratch(TM, D_MODEL),
                        f32_scratch(SUBLANES + TM, D_LRU),
                        f32_scratch(CONF_HALO + TM, D_CONF),
                        f32_scratch(TM, D_CONF),
                        f32_scratch(TM, D_LRU),
                        f32_scratch(SUBLANES, D_LRU),
                        f32_scratch(MIX_ROWS, D_LRU),
                        f32_scratch(MIX_ROWS, D_LRU),
                        bf16_scratch(TM, D_LRU + D_CONF),
                        bf16_scratch(TM, D_MODEL),
                        f32_scratch(SUBLANES, D_FF),
                        pltpu.VMEM((2, SUBLANES + TM, FF_CHUNK), _F32),
                        f32_scratch(TM, FF_CHUNK),
                        bf16_scratch(TM, D_FF)],
        compiler_params=pltpu.CompilerParams(dimension_semantics=("arbitrary",),
                                             vmem_limit_bytes=VMEM_LIMIT_LAYER),
        name="layer",
    )(x, k, v,
      row(mix_norm_g[0]), w_in, lru_conv_w[0], row(lru_conv_b[0]),
      _block_diag_gates(lru_w_a[0], lru_w_x[0]).astype(_BF16), row(lru_b_a[0]), row(lru_b_x[0]),
      row(lru_lambda[0]), conf_conv_w[0], row(conf_conv_b[0]), row(conf_ln_g[0]), row(conf_ln_b[0]), w_out,
      row(xa_norm_g[0]), w_q, w_o,
      row(ffn_norm_g[0]), w_up, ffn_conv_w[0], row(ffn_conv_b[0]), w_down, row(final_norm_g))
```

```python
import functools
import math

import jax
import jax.numpy as jnp
from jax import lax
from jax.experimental import pallas as pl
from jax.experimental.pallas import tpu as pltpu

D_MODEL = 1024
N_MEM = 256
D_LRU = 512
D_CONF = 512
LRU_HD = 64
LRU_CONV = 4
RG_C = 8.0
CONF_CONV = 31
XA_HEADS = 4
XA_HD = 256
D_FF = 3072
FFN_CONV = 3
EPS = 1e-6

SUBLANES = 8
HALF = 256
TM = 512
MIX_ROWS = TM
CONV_COLS = 256
FF_CHUNK = 512
CONF_HALO = 32
VMEM_LIMIT_KV = 32 * 1024 * 1024
VMEM_LIMIT_LAYER = 63 * 1024 * 1024

_BF16 = jnp.bfloat16
_F32 = jnp.float32
_DONE = object()


def _rms(x, g):
    return x * lax.rsqrt(jnp.mean(x * x, axis=-1, keepdims=True) + EPS) * g


_LOG2E = math.log2(math.e)


def _gelu_tanh_times(x, y):
    c = math.sqrt(2.0 / math.pi)
    k1 = -2.0 * c * _LOG2E
    k3 = k1 * 0.044715
    e = jnp.exp2(x * (k1 + k3 * (x * x)))
    return (x * y) / (1.0 + e)


def _sigmoid(x):
    return 1.0 / (1.0 + jnp.exp2(x * (-_LOG2E)))


def _dot(a, b):
    return jnp.dot(a, b, preferred_element_type=_F32)


def _shift_rows_1(x):
    return pltpu.roll(x, 1, 0)


def _kv_kernel(mem_ref, g_ref, wkv_ref, k_ref, v_ref):
    m = _rms(mem_ref[...], g_ref[...]).astype(_BF16)
    kv = _dot(m, wkv_ref[...].astype(_BF16))
    k_ref[...] = (kv[:, :D_MODEL] * (XA_HD ** -0.5)).astype(_BF16)
    v_ref[...] = kv[:, D_MODEL:].astype(_BF16)


def _mix_carry_in(first_tile, xbuf, cbuf, hcar):
    @pl.when(first_tile)
    def _():
        xbuf[0:SUBLANES, :] = jnp.zeros((SUBLANES, D_LRU), _F32)
        cbuf[0:CONF_HALO, :] = jnp.zeros((CONF_HALO, D_CONF), _F32)
        hcar[...] = jnp.zeros((SUBLANES, D_LRU), _F32)

    @pl.when(jnp.logical_not(first_tile))
    def _():
        xbuf[0:SUBLANES, :] = xbuf[TM:TM + SUBLANES, :]
        cbuf[0:CONF_HALO, :] = cbuf[TM:TM + CONF_HALO, :]


def _conv4(r0, cw4_ref, cb4_ref, xbuf):
    ext = xbuf[r0:r0 + MIX_ROWS + SUBLANES, :]
    acc = cw4_ref[0:1, :] * ext
    for k in range(1, LRU_CONV):
        acc = _shift_rows_1(acc) + cw4_ref[k:k + 1, :] * ext
    return acc[SUBLANES:, :] + cb4_ref[...]


def _conv31(r0, n_rows, cols, cw31_ref, cb31_ref, cbuf):
    ext_rows = n_rows + SUBLANES
    win = cbuf[r0:r0 + CONF_HALO + n_rows, cols]
    sources = (win, pltpu.roll(win, 4, 0))
    parts = [[None, None], [None, None]]
    for m in range(2):
        for r in range(2):
            for q in range(CONF_HALO // SUBLANES):
                for s in range(2):
                    d = 8 * q + 4 * s + 2 * m + r
                    if d >= CONF_CONV:
                        continue
                    k = CONF_CONV - 1 - d
                    start = CONF_HALO - SUBLANES - SUBLANES * q
                    term = cw31_ref[k:k + 1, cols] * sources[s][start:start + ext_rows, :]
                    parts[m][r] = term if parts[m][r] is None else parts[m][r] + term
    even = parts[0][0] + _shift_rows_1(parts[0][1])
    odd = parts[1][0] + _shift_rows_1(parts[1][1])
    acc = even + pltpu.roll(odd, 2, 0)
    return acc[SUBLANES:, :] + cb31_ref[:, cols]


def _recurrence(a, u, carry, abuf, ubuf):
    groups = MIX_ROWS // SUBLANES
    a3 = a.reshape(groups, SUBLANES, D_LRU)
    u3 = u.reshape(groups, SUBLANES, D_LRU)
    sub = lax.broadcasted_iota(jnp.int32, (groups, SUBLANES, D_LRU), 1)
    for k in (1, 2, 4):
        keep = sub >= k
        a_prev = jnp.where(keep, pltpu.roll(a3, k, 1), 1.0)
        u_prev = jnp.where(keep, pltpu.roll(u3, k, 1), 0.0)
        u3 = u3 + a3 * u_prev
        a3 = a3 * a_prev
    abuf[...] = a3.reshape(MIX_ROWS, D_LRU)
    ubuf[...] = u3.reshape(MIX_ROWS, D_LRU)
    for gidx in range(groups):
        rows = pl.ds(gidx * SUBLANES, SUBLANES)
        hg = ubuf[rows, :] + abuf[rows, :] * carry
        ubuf[rows, :] = hg
        carry = jnp.broadcast_to(hg[SUBLANES - 1:SUBLANES, :], (SUBLANES, D_LRU))
    return ubuf[...], carry


def _mix_block(r0, state, x_ref, g_ref, win_ref, cw4_ref, cb4_ref, wgate_ref, ba_ref, bx_ref, lam_ref,
               cw31_ref, cb31_ref, lng_ref, lnb_ref, wout_ref, o_ref,
               xbuf, cbuf, cbbuf, gatebuf, hcar, abuf, ubuf, ycat):
    rows = slice(r0, r0 + MIX_ROWS)
    h = _rms(x_ref[rows, :], g_ref[...]).astype(_BF16)
    z = _dot(h, win_ref[:, 0:2 * D_LRU])
    xbuf[SUBLANES + r0:SUBLANES + r0 + MIX_ROWS, :] = z[:, 0:D_LRU]
    gatebuf[rows, :] = z[:, D_LRU:]
    yield

    xc = _conv4(r0, cw4_ref, cb4_ref, xbuf)
    xcb = xc.astype(_BF16)
    for j in range(D_LRU // HALF):
        gj = _dot(xcb[:, j * HALF:(j + 1) * HALF], wgate_ref[j])
        abuf[:, j * HALF:(j + 1) * HALF] = gj[:, :HALF]
        ubuf[:, j * HALF:(j + 1) * HALF] = gj[:, HALF:]
    r = _sigmoid(abuf[...] + ba_ref[...])
    i = _sigmoid(ubuf[...] + bx_ref[...])
    nlam = -lam_ref[...]
    softplus = jnp.maximum(nlam, 0.0) + jnp.log1p(jnp.exp(-jnp.abs(nlam)))
    log_a = (-RG_C * softplus) * r
    a = jnp.exp(log_a)
    u = jnp.sqrt(-jnp.tanh(log_a) * (a * a + 1.0)) * (i * xc)
    assert state["next_row"] == r0, "row blocks must run their second piece in order"
    carry = hcar[...] if r0 == 0 else state["carry"]
    hseq, carry = _recurrence(a, u, carry, abuf, ubuf)
    state.update(next_row=r0 + MIX_ROWS, carry=carry)
    if r0 + MIX_ROWS == TM:
        hcar[...] = carry
    ycat[rows, 0:D_LRU] = _gelu_tanh_times(gatebuf[rows, :], hseq).astype(_BF16)

    brows = slice(CONF_HALO + r0, CONF_HALO + r0 + MIX_ROWS)
    for c0 in range(0, D_CONF, CONV_COLS):
        yield
        cols = slice(c0, c0 + CONV_COLS)
        cbuf[brows, cols] = _dot(h, win_ref[:, 2 * D_LRU + c0:2 * D_LRU + c0 + CONV_COLS])
        cbbuf[rows, cols] = _dot(h, win_ref[:, 2 * D_LRU + D_CONF + c0:2 * D_LRU + D_CONF + c0 + CONV_COLS])
        cbuf[brows, cols] = cbuf[brows, cols] * _sigmoid(cbbuf[rows, cols])
        cbbuf[rows, cols] = _conv31(r0, MIX_ROWS, cols, cw31_ref, cb31_ref, cbuf)
    c = cbbuf[rows, :]
    mu = jnp.mean(c, axis=-1, keepdims=True)
    cc = c - mu
    var = jnp.mean(cc * cc, axis=-1, keepdims=True)
    cn = cc * lax.rsqrt(var + EPS) * lng_ref[...] + lnb_ref[...]
    ycat[rows, D_LRU:] = (cn * _sigmoid(cn)).astype(_BF16)
    yield
    o_ref[rows, :] = _dot(ycat[rows, :], wout_ref[...])
    o_ref[rows, :] = o_ref[rows, :] + x_ref[rows, :]


def _xattn(x_ref, g_ref, wq_ref, k_ref, v_ref, wo_ref, o_ref, obuf, sbuf):
    h = _rms(x_ref[...], g_ref[...]).astype(_BF16)
    o_ref[...] = _dot(h, wq_ref[...])
    q = o_ref[...].astype(_BF16)
    yield
    heads = [slice(hd * XA_HD, (hd + 1) * XA_HD) for hd in range(XA_HEADS)]
    for hd, cols in enumerate(heads):
        sbuf[hd // 2, 0:TM, heads[hd % 2]] = lax.dot_general(
            q[:, cols], k_ref[:, cols], (((1,), (1,)), ((), ())), preferred_element_type=_F32)
    yield
    for hd, cols in enumerate(heads):
        s = sbuf[hd // 2, 0:TM, heads[hd % 2]]
        e = jnp.exp(s - jnp.max(s, axis=-1, keepdims=True))
        p = e * (1.0 / jnp.sum(e, axis=-1, keepdims=True))
        obuf[:, cols] = _dot(p.astype(_BF16), v_ref[:, cols]).astype(_BF16)
    yield
    o_ref[...] = _dot(obuf[...], wo_ref[...])
    o_ref[...] = o_ref[...] + x_ref[...]


def _ffn(x_ref, g_ref, wup_ref, cw_ref, cb_ref, wdown_ref, gf_ref, o_ref, ghalo, gbuf, ubuf2, actbuf):
    h = _rms(x_ref[...], g_ref[...]).astype(_BF16)
    for idx, c0 in enumerate(range(0, D_FF, FF_CHUNK)):
        cols = slice(c0, c0 + FF_CHUNK)
        slot = idx % 2
        gbuf[slot, 0:SUBLANES, :] = ghalo[:, cols]
        gbuf[slot, SUBLANES:SUBLANES + TM, :] = _dot(h, wup_ref[:, cols])
        ubuf2[...] = _dot(h, wup_ref[:, D_FF + c0:D_FF + c0 + FF_CHUNK])
        ghalo[:, cols] = gbuf[slot, TM:TM + SUBLANES, :]
        ext = gbuf[slot]
        acc = cw_ref[0:1, cols] * ext
        for k in range(1, FFN_CONV):
            acc = _shift_rows_1(acc) + cw_ref[k:k + 1, cols] * ext
        gc = acc[SUBLANES:, :] + cb_ref[:, cols]
        actbuf[:, cols] = _gelu_tanh_times(gc, ubuf2[...]).astype(_BF16)
        yield
    o_ref[...] = _dot(actbuf[...], wdown_ref[...])
    o_ref[...] = _rms(x_ref[...] + o_ref[...], gf_ref[...])


def _load_weights_as_bf16(weights, stages, sems):
    depth = len(stages)
    chunks = [(src, dst, r, c)
              for src, dst in weights
              for r in range(0, dst.shape[0], TM)
              for c in range(0, dst.shape[1], D_MODEL)]

    def chunk_copy(i):
        src, _, r, c = chunks[i]
        return pltpu.make_async_copy(src.at[0, pl.ds(r, TM), pl.ds(c, D_MODEL)], stages[i % depth],
                                     sems.at[i % depth])

    for i in range(min(depth - 1, len(chunks))):
        chunk_copy(i).start()
    for i, (_, dst, r, c) in enumerate(chunks):
        if i + depth - 1 < len(chunks):
            chunk_copy(i + depth - 1).start()
        chunk_copy(i).wait()
        dst[r:r + TM, c:c + D_MODEL] = stages[i % depth][...].astype(_BF16)


def _layer_kernel(tiles_per_seq,
                  x_ref, k_ref, v_ref,
                  g_mix, win_hbm, cw4_ref, cb4_ref, wgate_ref, ba_ref, bx_ref, lam_ref,
                  cw31_ref, cb31_ref, lng_ref, lnb_ref, wout_hbm,
                  g_xa, wq_hbm, wo_hbm,
                  g_ffn, wup_hbm, cwf_ref, cbf_ref, wdown_hbm, g_final,
                  o_ref,
                  win_ref, wout_ref, wq_ref, wo_ref, wup_ref, wdown_ref, wsem, wstage,
                  x1buf, x2buf, xbuf, cbuf, cbbuf, gatebuf, hcar, abuf, ubuf, ycat, obuf, ghalo, gbuf, ubuf2, actbuf):
    s = pl.program_id(0)
    n_tiles = pl.num_programs(0) - 1
    mix_tile = jnp.minimum(s, n_tiles - 1)
    out_tile = jnp.maximum(s - 1, 0)

    @pl.when(s == 0)
    def _():
        _load_weights_as_bf16([(win_hbm, win_ref), (wq_hbm, wq_ref), (wo_hbm, wo_ref), (wup_hbm, wup_ref),
                               (wdown_hbm, wdown_ref), (wout_hbm, wout_ref)], (x1buf, x2buf, wstage), wsem)
        x1buf[...] = jnp.zeros((TM, D_MODEL), _F32)

    @pl.when(out_tile % tiles_per_seq == 0)
    def _():
        ghalo[...] = jnp.zeros((SUBLANES, D_FF), _F32)

    _mix_carry_in(mix_tile % tiles_per_seq == 0, xbuf, cbuf, hcar)

    state = {"next_row": 0, "carry": None}
    m0 = _mix_block(0, state, x_ref, g_mix, win_ref, cw4_ref, cb4_ref, wgate_ref, ba_ref, bx_ref, lam_ref,
                    cw31_ref, cb31_ref, lng_ref, lnb_ref, wout_ref, x1buf,
                    xbuf, cbuf, cbbuf, gatebuf, hcar, abuf, ubuf, ycat)
    xa = _xattn(x1buf, g_xa, wq_ref, k_ref, v_ref, wo_ref, x2buf, obuf, gbuf)
    ff = _ffn(x2buf, g_ffn, wup_ref, cwf_ref, cbf_ref, wdown_ref, g_final, o_ref, ghalo, gbuf, ubuf2, actbuf)
    schedule = [m0, xa,
                xa, m0, xa,
                xa, m0,
                ff, ff, ff, ff, ff, ff,
                m0,
                ff, m0]
    live = {id(g): True for g in (m0, xa, ff)}
    for g in schedule:
        live[id(g)] = next(g, _DONE) is not _DONE
    assert not any(live.values()), "schedule must exhaust every stage"


def _const_spec(shape):
    nd = len(shape)
    return pl.BlockSpec(shape, lambda s: (0,) * nd, pipeline_mode=pl.Buffered(1))


def _block_diag_gates(w_a, w_x):
    heads_per_half = HALF // LRU_HD
    eye = jnp.eye(heads_per_half, dtype=w_a.dtype)

    def bd(w):
        w = w.reshape(D_LRU // HALF, heads_per_half, LRU_HD, LRU_HD)
        return jnp.einsum("jhik,hg->jhigk", w, eye).reshape(D_LRU // HALF, HALF, HALF)

    return jnp.concatenate([bd(w_a), bd(w_x)], axis=-1)


def kernel(x, mem, mix_norm_g, w_in, lru_conv_w, lru_conv_b, lru_w_a, lru_b_a, lru_w_x, lru_b_x, lru_lambda, conf_conv_w, conf_conv_b, conf_ln_g, conf_ln_b, w_out, xa_norm_g, mem_norm_g, w_q, w_kv, w_o, ffn_norm_g, w_up, ffn_conv_w, ffn_conv_b, w_down, final_norm_g):
    bsz, seq, d = x.shape
    assert d == D_MODEL and seq % TM == 0 and mem.shape == (bsz, N_MEM, D_MODEL)
    assert w_in.shape[0] == 1
    tiles_per_seq = seq // TM
    n_tiles = bsz * tiles_per_seq
    row = lambda v: v.reshape(1, -1)

    k, v = pl.pallas_call(
        _kv_kernel,
        grid=(bsz,),
        in_specs=[pl.BlockSpec((None, N_MEM, D_MODEL), lambda b: (b, 0, 0)),
                  pl.BlockSpec((1, D_MODEL), lambda b: (0, 0)),
                  pl.BlockSpec((None, D_MODEL, 2 * D_MODEL), lambda b: (0, 0, 0), pipeline_mode=pl.Buffered(1))],
        out_specs=[pl.BlockSpec((None, N_MEM, D_MODEL), lambda b: (b, 0, 0))] * 2,
        out_shape=[jax.ShapeDtypeStruct((bsz, N_MEM, D_MODEL), _BF16)] * 2,
        compiler_params=pltpu.CompilerParams(dimension_semantics=("arbitrary",),
                                             vmem_limit_bytes=VMEM_LIMIT_KV),
        name="kv_proj",
    )(mem, row(mem_norm_g[0]), w_kv)

    def mix_rows(s):
        j = jnp.minimum(s, n_tiles - 1)
        return (j // tiles_per_seq, j % tiles_per_seq, 0)

    def out_rows(s):
        j = jnp.maximum(s - 1, 0)
        return (j // tiles_per_seq, j % tiles_per_seq, 0)

    kv_spec = pl.BlockSpec((None, N_MEM, D_MODEL), lambda s: (jnp.maximum(s - 1, 0) // tiles_per_seq, 0, 0),
                           pipeline_mode=pl.Buffered(1))
    f32_scratch = lambda rows, cols: pltpu.VMEM((rows, cols), _F32)
    bf16_scratch = lambda rows, cols: pltpu.VMEM((rows, cols), _BF16)
    hbm = pl.BlockSpec(memory_space=pl.ANY)
    return pl.pallas_call(
        functools.partial(_layer_kernel, tiles_per_seq),
        grid=(n_tiles + 1,),
        in_specs=[pl.BlockSpec((None, TM, D_MODEL), mix_rows), kv_spec, kv_spec,
                  _const_spec((1, D_MODEL)),
                  hbm,
                  _const_spec((LRU_CONV, D_LRU)),
                  _const_spec((1, D_LRU)),
                  _const_spec((D_LRU // HALF, HALF, 2 * HALF)),
                  _const_spec((1, D_LRU)),
                  _const_spec((1, D_LRU)),
                  _const_spec((1, D_LRU)),
                  _const_spec((CONF_CONV, D_CONF)),
                  _const_spec((1, D_CONF)),
                  _const_spec((1, D_CONF)),
                  _const_spec((1, D_CONF)),
                  hbm,
                  _const_spec((1, D_MODEL)),
                  hbm, hbm,
                  _const_spec((1, D_MODEL)),
                  hbm,
                  _const_spec((FFN_CONV, D_FF)),
                  _const_spec((1, D_FF)),
                  hbm,
                  _const_spec((1, D_MODEL))],
        out_specs=pl.BlockSpec((None, TM, D_MODEL), out_rows),
        out_shape=jax.ShapeDtypeStruct(x.shape, x.dtype),
        scratch_shapes=[bf16_scratch(D_MODEL, 2 * (D_LRU + D_CONF)),
                        bf16_scratch(D_LRU + D_CONF, D_MODEL),
                        bf16_scratch(D_MODEL, D_MODEL),
                        bf16_scratch(D_MODEL, D_MODEL),
                        bf16_scratch(D_MODEL, 2 * D_FF),
                        bf16_scratch(D_FF, D_MODEL),
                        pltpu.SemaphoreType.DMA((3,)),
                        f32_scratch(TM, D_MODEL),
                        f32_scratch(TM, D_MODEL),
                        f32_scratch(TM, D_MODEL),
                        f32_scratch(SUBLANES + TM, D_LRU),
                        f32_scratch(CONF_HALO + TM, D_CONF),
                        f32_scratch(TM, D_CONF),
                        f32_scratch(TM, D_LRU),
                        f32_scratch(SUBLANES, D_LRU),
                        f32_scratch(MIX_ROWS, D_LRU),
                        f32_scratch(MIX_ROWS, D_LRU),
                        bf16_scratch(TM, D_LRU + D_CONF),
                        bf16_scratch(TM, D_MODEL),
                        f32_scratch(SUBLANES, D_FF),
                        pltpu.VMEM((2, SUBLANES + TM, FF_CHUNK), _F32),
                        f32_scratch(TM, FF_CHUNK),
                        bf16_scratch(TM, D_FF)],
        compiler_params=pltpu.CompilerParams(dimension_semantics=("arbitrary",),
                                             vmem_limit_bytes=VMEM_LIMIT_LAYER),
        name="layer",
    )(x, k, v,
      row(mix_norm_g[0]), w_in, lru_conv_w[0], row(lru_conv_b[0]),
      _block_diag_gates(lru_w_a[0], lru_w_x[0]).astype(_BF16), row(lru_b_a[0]), row(lru_b_x[0]),
      row(lru_lambda[0]), conf_conv_w[0], row(conf_conv_b[0]), row(conf_ln_g[0]), row(conf_ln_b[0]), w_out,
      row(xa_norm_g[0]), w_q, w_o,
      row(ffn_norm_g[0]), w_up, ffn_conv_w[0], row(ffn_conv_b[0]), w_down, row(final_norm_g))
```

```python
import functools
import math

import jax
import jax.numpy as jnp
from jax import lax
from jax.experimental import pallas as pl
from jax.experimental.pallas import tpu as pltpu

D_MODEL = 1024
N_MEM = 256
D_LRU = 512
D_CONF = 512
LRU_HD = 64
LRU_CONV = 4
RG_C = 8.0
CONF_CONV = 31
XA_HEADS = 4
XA_HD = 256
D_FF = 3072
FFN_CONV = 3
EPS = 1e-6

SUBLANES = 8
HALF = 256
TM = 512
MIX_ROWS = TM
CONV_COLS = 256
FF_CHUNK = 512
CONF_HALO = 32
VMEM_LIMIT_KV = 32 * 1024 * 1024
VMEM_LIMIT_LAYER = 63 * 1024 * 1024

_BF16 = jnp.bfloat16
_F32 = jnp.float32
_DONE = object()


def _rms(x, g):
    return x * lax.rsqrt(jnp.mean(x * x, axis=-1, keepdims=True) + EPS) * g


_LOG2E = math.log2(math.e)


def _gelu_tanh_times(x, y):
    c = math.sqrt(2.0 / math.pi)
    k1 = -2.0 * c * _LOG2E
    k3 = k1 * 0.044715
    e = jnp.exp2(x * (k1 + k3 * (x * x)))
    return (x * y) / (1.0 + e)


def _sigmoid(x):
    return 1.0 / (1.0 + jnp.exp2(x * (-_LOG2E)))


def _dot(a, b):
    return jnp.dot(a, b, preferred_element_type=_F32)


def _shift_rows_1(x):
    return pltpu.roll(x, 1, 0)


def _kv_kernel(mem_ref, g_ref, wkv_ref, k_ref, v_ref):
    m = _rms(mem_ref[...], g_ref[...]).astype(_BF16)
    kv = _dot(m, wkv_ref[...].astype(_BF16))
    k_ref[...] = (kv[:, :D_MODEL] * (XA_HD ** -0.5)).astype(_BF16)
    v_ref[...] = kv[:, D_MODEL:].astype(_BF16)


def _mix_carry_in(first_tile, xbuf, cbuf, hcar):
    @pl.when(first_tile)
    def _():
        xbuf[0:SUBLANES, :] = jnp.zeros((SUBLANES, D_LRU), _F32)
        cbuf[0:CONF_HALO, :] = jnp.zeros((CONF_HALO, D_CONF), _F32)
        hcar[...] = jnp.zeros((SUBLANES, D_LRU), _F32)

    @pl.when(jnp.logical_not(first_tile))
    def _():
        xbuf[0:SUBLANES, :] = xbuf[TM:TM + SUBLANES, :]
        cbuf[0:CONF_HALO, :] = cbuf[TM:TM + CONF_HALO, :]


def _conv4(r0, cw4_ref, cb4_ref, xbuf):
    ext = xbuf[r0:r0 + MIX_ROWS + SUBLANES, :]
    acc = cw4_ref[0:1, :] * ext
    for k in range(1, LRU_CONV):
        acc = _shift_rows_1(acc) + cw4_ref[k:k + 1, :] * ext
    return acc[SUBLANES:, :] + cb4_ref[...]


def _conv31(r0, n_rows, cols, cw31_ref, cb31_ref, cbuf):
    ext_rows = n_rows + SUBLANES
    win = cbuf[r0:r0 + CONF_HALO + n_rows, cols]
    sources = (win, pltpu.roll(win, 4, 0))
    parts = [[None, None], [None, None]]
    for m in range(2):
        for r in range(2):
            for q in range(CONF_HALO // SUBLANES):
                for s in range(2):
                    d = 8 * q + 4 * s + 2 * m + r
                    if d >= CONF_CONV:
                        continue
                    k = CONF_CONV - 1 - d
                    start = CONF_HALO - SUBLANES - SUBLANES * q
                    term = cw31_ref[k:k + 1, cols] * sources[s][start:start + ext_rows, :]
                    parts[m][r] = term if parts[m][r] is None else parts[m][r] + term
    even = parts[0][0] + _shift_rows_1(parts[0][1])
    odd = parts[1][0] + _shift_rows_1(parts[1][1])
    acc = even + pltpu.roll(odd, 2, 0)
    return acc[SUBLANES:, :] + cb31_ref[:, cols]


def _recurrence(a, u, carry, abuf, ubuf):
    groups = MIX_ROWS // SUBLANES
    a3 = a.reshape(groups, SUBLANES, D_LRU)
    u3 = u.reshape(groups, SUBLANES, D_LRU)
    sub = lax.broadcasted_iota(jnp.int32, (groups, SUBLANES, D_LRU), 1)
    for k in (1, 2, 4):
        keep = sub >= k
        a_prev = jnp.where(keep, pltpu.roll(a3, k, 1), 1.0)
        u_prev = jnp.where(keep, pltpu.roll(u3, k, 1), 0.0)
        u3 = u3 + a3 * u_prev
        a3 = a3 * a_prev
    abuf[...] = a3.reshape(MIX_ROWS, D_LRU)
    ubuf[...] = u3.reshape(MIX_ROWS, D_LRU)
    for gidx in range(groups):
        rows = pl.ds(gidx * SUBLANES, SUBLANES)
        hg = ubuf[rows, :] + abuf[rows, :] * carry
        ubuf[rows, :] = hg
        carry = jnp.broadcast_to(hg[SUBLANES - 1:SUBLANES, :], (SUBLANES, D_LRU))
    return ubuf[...], carry


def _mix_block(r0, state, x_ref, g_ref, win_ref, cw4_ref, cb4_ref, wgate_ref, ba_ref, bx_ref, lam_ref,
               cw31_ref, cb31_ref, lng_ref, lnb_ref, wout_ref, o_ref,
               xbuf, cbuf, cbbuf, gatebuf, hcar, abuf, ubuf, ycat):
    rows = slice(r0, r0 + MIX_ROWS)
    h = _rms(x_ref[rows, :], g_ref[...]).astype(_BF16)
    z = _dot(h, win_ref[:, 0:2 * D_LRU])
    xbuf[SUBLANES + r0:SUBLANES + r0 + MIX_ROWS, :] = z[:, 0:D_LRU]
    gatebuf[rows, :] = z[:, D_LRU:]
    yield

    xc = _conv4(r0, cw4_ref, cb4_ref, xbuf)
    xcb = xc.astype(_BF16)
    for j in range(D_LRU // HALF):
        gj = _dot(xcb[:, j * HALF:(j + 1) * HALF], wgate_ref[j])
        abuf[:, j * HALF:(j + 1) * HALF] = gj[:, :HALF]
        ubuf[:, j * HALF:(j + 1) * HALF] = gj[:, HALF:]
    r = _sigmoid(abuf[...] + ba_ref[...])
    i = _sigmoid(ubuf[...] + bx_ref[...])
    nlam = -lam_ref[...]
    softplus = jnp.maximum(nlam, 0.0) + jnp.log1p(jnp.exp(-jnp.abs(nlam)))
    log_a = (-RG_C * softplus) * r
    a = jnp.exp(log_a)
    u = jnp.sqrt(-jnp.tanh(log_a) * (a * a + 1.0)) * (i * xc)
    assert state["next_row"] == r0, "row blocks must run their second piece in order"
    carry = hcar[...] if r0 == 0 else state["carry"]
    hseq, carry = _recurrence(a, u, carry, abuf, ubuf)
    state.update(next_row=r0 + MIX_ROWS, carry=carry)
    if r0 + MIX_ROWS == TM:
        hcar[...] = carry
    ycat[rows, 0:D_LRU] = _gelu_tanh_times(gatebuf[rows, :], hseq).astype(_BF16)

    brows = slice(CONF_HALO + r0, CONF_HALO + r0 + MIX_ROWS)
    for c0 in range(0, D_CONF, CONV_COLS):
        yield
        cols = slice(c0, c0 + CONV_COLS)
        cbuf[brows, cols] = _dot(h, win_ref[:, 2 * D_LRU + c0:2 * D_LRU + c0 + CONV_COLS])
        cbbuf[rows, cols] = _dot(h, win_ref[:, 2 * D_LRU + D_CONF + c0:2 * D_LRU + D_CONF + c0 + CONV_COLS])
        cbuf[brows, cols] = cbuf[brows, cols] * _sigmoid(cbbuf[rows, cols])
        cbbuf[rows, cols] = _conv31(r0, MIX_ROWS, cols, cw31_ref, cb31_ref, cbuf)
    c = cbbuf[rows, :]
    mu = jnp.mean(c, axis=-1, keepdims=True)
    cc = c - mu
    var = jnp.mean(cc * cc, axis=-1, keepdims=True)
    cn = cc * lax.rsqrt(var + EPS) * lng_ref[...] + lnb_ref[...]
    ycat[rows, D_LRU:] = (cn * _sigmoid(cn)).astype(_BF16)
    yield
    o_ref[rows, :] = _dot(ycat[rows, :], wout_ref[...])
    o_ref[rows, :] = o_ref[rows, :] + x_ref[rows, :]


def _xattn(x_ref, g_ref, wq_ref, k_ref, v_ref, wo_ref, o_ref, obuf, sbuf):
    h = _rms(x_ref[...], g_ref[...]).astype(_BF16)
    o_ref[...] = _dot(h, wq_ref[...])
    q = o_ref[...].astype(_BF16)
    yield
    heads = [slice(hd * XA_HD, (hd + 1) * XA_HD) for hd in range(XA_HEADS)]
    for hd, cols in enumerate(heads):
        sbuf[hd // 2, 0:TM, heads[hd % 2]] = lax.dot_general(
            q[:, cols], k_ref[:, cols], (((1,), (1,)), ((), ())), preferred_element_type=_F32)
    yield
    for hd, cols in enumerate(heads):
        s = sbuf[hd // 2, 0:TM, heads[hd % 2]]
        e = jnp.exp(s - jnp.max(s, axis=-1, keepdims=True))
        p = e * (1.0 / jnp.sum(e, axis=-1, keepdims=True))
        obuf[:, cols] = _dot(p.astype(_BF16), v_ref[:, cols]).astype(_BF16)
    yield
    o_ref[...] = _dot(obuf[...], wo_ref[...])
    o_ref[...] = o_ref[...] + x_ref[...]


def _ffn(x_ref, g_ref, wup_ref, cw_ref, cb_ref, wdown_ref, gf_ref, o_ref, ghalo, gbuf, ubuf2, actbuf):
    h = _rms(x_ref[...], g_ref[...]).astype(_BF16)
    for idx, c0 in enumerate(range(0, D_FF, FF_CHUNK)):
        cols = slice(c0, c0 + FF_CHUNK)
        slot = idx % 2
        gbuf[slot, 0:SUBLANES, :] = ghalo[:, cols]
        gbuf[slot, SUBLANES:SUBLANES + TM, :] = _dot(h, wup_ref[:, cols])
        ubuf2[...] = _dot(h, wup_ref[:, D_FF + c0:D_FF + c0 + FF_CHUNK])
        ghalo[:, cols] = gbuf[slot, TM:TM + SUBLANES, :]
        ext = gbuf[slot]
        acc = cw_ref[0:1, cols] * ext
        for k in range(1, FFN_CONV):
            acc = _shift_rows_1(acc) + cw_ref[k:k + 1, cols] * ext
        gc = acc[SUBLANES:, :] + cb_ref[:, cols]
        actbuf[:, cols] = _gelu_tanh_times(gc, ubuf2[...]).astype(_BF16)
        yield
    o_ref[...] = _dot(actbuf[...], wdown_ref[...])
    o_ref[...] = _rms(x_ref[...] + o_ref[...], gf_ref[...])


def _load_weights_as_bf16(weights, stages, sems):
    depth = len(stages)
    rows = stages[0].shape[0]
    chunks = [(src, dst, r, c)
              for src, dst in weights
              for r in range(0, dst.shape[0], rows)
              for c in range(0, dst.shape[1], D_MODEL)]

    def chunk_copy(i):
        src, _, r, c = chunks[i]
        return pltpu.make_async_copy(src.at[0, pl.ds(r, rows), pl.ds(c, D_MODEL)], stages[i % depth],
                                     sems.at[i % depth])

    for i in range(min(depth - 1, len(chunks))):
        chunk_copy(i).start()
    for i, (_, dst, r, c) in enumerate(chunks):
        if i + depth - 1 < len(chunks):
            chunk_copy(i + depth - 1).start()
        chunk_copy(i).wait()
        dst[r:r + rows, c:c + D_MODEL] = stages[i % depth][...].astype(_BF16)


def _layer_kernel(tiles_per_seq,
                  x_ref, k_ref, v_ref,
                  g_mix, win_hbm, cw4_ref, cb4_ref, wgate_ref, ba_ref, bx_ref, lam_ref,
                  cw31_ref, cb31_ref, lng_ref, lnb_ref, wout_hbm,
                  g_xa, wq_hbm, wo_hbm,
                  g_ffn, wup_hbm, cwf_ref, cbf_ref, wdown_hbm, g_final,
                  o_ref,
                  win_ref, wout_ref, wq_ref, wo_ref, wup_ref, wdown_ref, wsem, wstage,
                  x1buf, x2buf, xbuf, cbuf, cbbuf, gatebuf, hcar, abuf, ubuf, ycat, obuf, ghalo, gbuf, ubuf2, actbuf):
    s = pl.program_id(0)
    n_tiles = pl.num_programs(0) - 1
    mix_tile = jnp.minimum(s, n_tiles - 1)
    out_tile = jnp.maximum(s - 1, 0)

    @pl.when(s == 0)
    def _():
        _load_weights_as_bf16([(win_hbm, win_ref), (wq_hbm, wq_ref), (wo_hbm, wo_ref), (wup_hbm, wup_ref),
                               (wdown_hbm, wdown_ref), (wout_hbm, wout_ref)],
                              [buf.at[pl.ds(r, TM // 2)] for buf in (x1buf, x2buf, wstage) for r in (0, TM // 2)], wsem)
        x1buf[...] = jnp.zeros((TM, D_MODEL), _F32)

    @pl.when(out_tile % tiles_per_seq == 0)
    def _():
        ghalo[...] = jnp.zeros((SUBLANES, D_FF), _F32)

    _mix_carry_in(mix_tile % tiles_per_seq == 0, xbuf, cbuf, hcar)

    state = {"next_row": 0, "carry": None}
    m0 = _mix_block(0, state, x_ref, g_mix, win_ref, cw4_ref, cb4_ref, wgate_ref, ba_ref, bx_ref, lam_ref,
                    cw31_ref, cb31_ref, lng_ref, lnb_ref, wout_ref, x1buf,
                    xbuf, cbuf, cbbuf, gatebuf, hcar, abuf, ubuf, ycat)
    xa = _xattn(x1buf, g_xa, wq_ref, k_ref, v_ref, wo_ref, x2buf, obuf, gbuf)
    ff = _ffn(x2buf, g_ffn, wup_ref, cwf_ref, cbf_ref, wdown_ref, g_final, o_ref, ghalo, gbuf, ubuf2, actbuf)
    schedule = [m0, xa,
                xa, m0, xa,
                xa, m0,
                ff, ff, ff, ff, ff, ff,
                m0,
                ff, m0]
    live = {id(g): True for g in (m0, xa, ff)}
    for g in schedule:
        live[id(g)] = next(g, _DONE) is not _DONE
    assert not any(live.values()), "schedule must exhaust every stage"


def _const_spec(shape):
    nd = len(shape)
    return pl.BlockSpec(shape, lambda s: (0,) * nd, pipeline_mode=pl.Buffered(1))


def _block_diag_gates(w_a, w_x):
    heads_per_half = HALF // LRU_HD
    eye = jnp.eye(heads_per_half, dtype=w_a.dtype)

    def bd(w):
        w = w.reshape(D_LRU // HALF, heads_per_half, LRU_HD, LRU_HD)
        return jnp.einsum("jhik,hg->jhigk", w, eye).reshape(D_LRU // HALF, HALF, HALF)

    return jnp.concatenate([bd(w_a), bd(w_x)], axis=-1)


def kernel(x, mem, mix_norm_g, w_in, lru_conv_w, lru_conv_b, lru_w_a, lru_b_a, lru_w_x, lru_b_x, lru_lambda, conf_conv_w, conf_conv_b, conf_ln_g, conf_ln_b, w_out, xa_norm_g, mem_norm_g, w_q, w_kv, w_o, ffn_norm_g, w_up, ffn_conv_w, ffn_conv_b, w_down, final_norm_g):
    bsz, seq, d = x.shape
    assert d == D_MODEL and seq % TM == 0 and mem.shape == (bsz, N_MEM, D_MODEL)
    assert w_in.shape[0] == 1
    tiles_per_seq = seq // TM
    n_tiles = bsz * tiles_per_seq
    row = lambda v: v.reshape(1, -1)

    k, v = pl.pallas_call(
        _kv_kernel,
        grid=(bsz,),
        in_specs=[pl.BlockSpec((None, N_MEM, D_MODEL), lambda b: (b, 0, 0)),
                  pl.BlockSpec((1, D_MODEL), lambda b: (0, 0)),
                  pl.BlockSpec((None, D_MODEL, 2 * D_MODEL), lambda b: (0, 0, 0), pipeline_mode=pl.Buffered(1))],
        out_specs=[pl.BlockSpec((None, N_MEM, D_MODEL), lambda b: (b, 0, 0))] * 2,
        out_shape=[jax.ShapeDtypeStruct((bsz, N_MEM, D_MODEL), _BF16)] * 2,
        compiler_params=pltpu.CompilerParams(dimension_semantics=("arbitrary",),
                                             vmem_limit_bytes=VMEM_LIMIT_KV),
        name="kv_proj",
    )(mem, row(mem_norm_g[0]), w_kv)

    def mix_rows(s):
        j = jnp.minimum(s, n_tiles - 1)
        return (j // tiles_per_seq, j % tiles_per_seq, 0)

    def out_rows(s):
        j = jnp.maximum(s - 1, 0)
        return (j // tiles_per_seq, j % tiles_per_seq, 0)

    kv_spec = pl.BlockSpec((None, N_MEM, D_MODEL), lambda s: (jnp.maximum(s - 1, 0) // tiles_per_seq, 0, 0),
                           pipeline_mode=pl.Buffered(1))
    f32_scratch = lambda rows, cols: pltpu.VMEM((rows, cols), _F32)
    bf16_scratch = lambda rows, cols: pltpu.VMEM((rows, cols), _BF16)
    hbm = pl.BlockSpec(memory_space=pl.ANY)
    return pl.pallas_call(
        functools.partial(_layer_kernel, tiles_per_seq),
        grid=(n_tiles + 1,),
        in_specs=[pl.BlockSpec((None, TM, D_MODEL), mix_rows), kv_spec, kv_spec,
                  _const_spec((1, D_MODEL)),
                  hbm,
                  _const_spec((LRU_CONV, D_LRU)),
                  _const_spec((1, D_LRU)),
                  _const_spec((D_LRU // HALF, HALF, 2 * HALF)),
                  _const_spec((1, D_LRU)),
                  _const_spec((1, D_LRU)),
                  _const_spec((1, D_LRU)),
                  _const_spec((CONF_CONV, D_CONF)),
                  _const_spec((1, D_CONF)),
                  _const_spec((1, D_CONF)),
                  _const_spec((1, D_CONF)),
                  hbm,
                  _const_spec((1, D_MODEL)),
                  hbm, hbm,
                  _const_spec((1, D_MODEL)),
                  hbm,
                  _const_spec((FFN_CONV, D_FF)),
                  _const_spec((1, D_FF)),
                  hbm,
                  _const_spec((1, D_MODEL))],
        out_specs=pl.BlockSpec((None, TM, D_MODEL), out_rows),
        out_shape=jax.ShapeDtypeStruct(x.shape, x.dtype),
        scratch_shapes=[bf16_scratch(D_MODEL, 2 * (D_LRU + D_CONF)),
                        bf16_scratch(D_LRU + D_CONF, D_MODEL),
                        bf16_scratch(D_MODEL, D_MODEL),
                        bf16_scratch(D_MODEL, D_MODEL),
                        bf16_scratch(D_MODEL, 2 * D_FF),
                        bf16_scratch(D_FF, D_MODEL),
                        pltpu.SemaphoreType.DMA((6,)),
                        f32_scratch(TM, D_MODEL),
                        f32_scratch(TM, D_MODEL),
                        f32_scratch(TM, D_MODEL),
                        f32_scratch(SUBLANES + TM, D_LRU),
                        f32_scratch(CONF_HALO + TM, D_CONF),
                        f32_scratch(TM, D_CONF),
                        f32_scratch(TM, D_LRU),
                        f32_scratch(SUBLANES, D_LRU),
                        f32_scratch(MIX_ROWS, D_LRU),
                        f32_scratch(MIX_ROWS, D_LRU),
                        bf16_scratch(TM, D_LRU + D_CONF),
                        bf16_scratch(TM, D_MODEL),
                        f32_scratch(SUBLANES, D_FF),
                        pltpu.VMEM((2, SUBLANES + TM, FF_CHUNK), _F32),
                        f32_scratch(TM, FF_CHUNK),
                        bf16_scratch(TM, D_FF)],
        compiler_params=pltpu.CompilerParams(dimension_semantics=("arbitrary",),
                                             vmem_limit_bytes=VMEM_LIMIT_LAYER),
        name="layer",
    )(x, k, v,
      row(mix_norm_g[0]), w_in, lru_conv_w[0], row(lru_conv_b[0]),
      _block_diag_gates(lru_w_a[0], lru_w_x[0]).astype(_BF16), row(lru_b_a[0]), row(lru_b_x[0]),
      row(lru_lambda[0]), conf_conv_w[0], row(conf_conv_b[0]), row(conf_ln_g[0]), row(conf_ln_b[0]), w_out,
      row(xa_norm_g[0]), w_q, w_o,
      row(ffn_norm_g[0]), w_up, ffn_conv_w[0], row(ffn_conv_b[0]), w_down, row(final_norm_g))
```
